```python
import jax, jax.numpy as jnp
from jax import lax
import numpy as np

D_MODEL = 1024
BATCH = 2
SEQ = 8192
DEPTH = 4
DEC_BATCH = 128
DEC_SEQ = 4
PAST_LEN = 2048
PAGE_SIZE = 128

N_A_LAYERS = DEPTH // 2
N_B_LAYERS = DEPTH - N_A_LAYERS
N_HEADS = 16
HEAD_DIM = D_MODEL // N_HEADS
ATTN_WIDTH = N_HEADS * HEAD_DIM
WINDOWS = (128, 512, 2048)
DILATIONS = (1, 4, 16)
N_BRANCH = 3
D_FF = ((8 * D_MODEL + 3 * 256 - 1) // (3 * 256)) * 256
QBLK = 128
RMS_EPS = 1e-6
FORGET_BIAS_INIT = 2.0

kernel_name = 'yoco_fox_dilated_decoder'


def rmsnorm(x, g):
    xf = x.astype(jnp.float32)
    y = xf * lax.rsqrt(jnp.mean(xf * xf, axis=-1, keepdims=True) + RMS_EPS)
    return (y * g.astype(jnp.float32)).astype(x.dtype)


def swiglu(x, w_gate_up, w_down):
    gate, up = jnp.split(x @ w_gate_up, 2, axis=-1)
    return (jax.nn.silu(gate) * up) @ w_down


def alibi_slopes():
    return jnp.exp2(-8.0 * jnp.arange(1, N_HEADS + 1, dtype=jnp.float32) / N_HEADS)


def fox_project(h, w_qkvf, b_f):
    B, S, _ = h.shape
    p = h @ w_qkvf
    qkv = p[..., :3 * ATTN_WIDTH].reshape(B, S, 3, N_HEADS, HEAD_DIM)
    logf = jax.nn.log_sigmoid((p[..., 3 * ATTN_WIDTH:] + b_f).astype(jnp.float32))
    return qkv[:, :, 0], qkv[:, :, 1], qkv[:, :, 2], logf


def fox_prompt(q, k, v, logf):
    B, S = q.shape[0], q.shape[1]
    scale = HEAD_DIM ** -0.5
    C = jnp.cumsum(logf, axis=1).transpose(0, 2, 1)
    key_pos = jnp.arange(S)

    def block(i):
        t0 = i * QBLK
        qb = lax.dynamic_slice_in_dim(q, t0, QBLK, axis=1)
        cq = lax.dynamic_slice_in_dim(C, t0, QBLK, axis=2)
        s = jnp.einsum('bqhd,bshd->bhqs', qb, k).astype(jnp.float32) * scale
        s = s + cq[..., :, None] - C[..., None, :]
        causal = key_pos[None, :] <= (t0 + jnp.arange(QBLK))[:, None]
        s = jnp.where(causal, s, -jnp.inf)
        p = jax.nn.softmax(s, axis=-1).astype(v.dtype)
        return jnp.einsum('bhqs,bshd->bqhd', p, v)

    out = lax.map(block, jnp.arange(S // QBLK))
    return out.transpose(1, 0, 2, 3, 4).reshape(B, S, ATTN_WIDTH)


def fox_sample(q, k, v, logf, k_past, v_past, logf_past):
    DB, T = q.shape[0], q.shape[1]
    P = k_past.shape[1]
    scale = HEAD_DIM ** -0.5
    C_past = jnp.cumsum(logf_past.astype(jnp.float32), axis=1)
    C_new = C_past[:, -1:] + jnp.cumsum(logf, axis=1)
    cq = C_new.transpose(0, 2, 1)
    s_past = jnp.einsum('bqhd,bshd->bhqs', q, k_past).astype(jnp.float32) * scale
    s_past = s_past + cq[..., :, None] - C_past.transpose(0, 2, 1)[:, :, None, :]
    s_new = jnp.einsum('bqhd,bshd->bhqs', q, k).astype(jnp.float32) * scale
    s_new = s_new + cq[..., :, None] - cq[..., None, :]
    causal = jnp.arange(T)[None, :] <= jnp.arange(T)[:, None]
    s_new = jnp.where(causal, s_new, -jnp.inf)
    p = jax.nn.softmax(jnp.concatenate([s_past, s_new], axis=-1), axis=-1).astype(v.dtype)
    o = jnp.einsum('bhqs,bshd->bqhd', p[..., :P], v_past) + jnp.einsum('bhqs,bshd->bqhd', p[..., P:], v)
    return o.reshape(DB, T, ATTN_WIDTH)


def dilated_branch(q, k_ext, v_ext, q_idx, dil, n_taps, slopes):
    offs = jnp.arange(n_taps, dtype=jnp.int32) * dil
    kidx = q_idx[:, None] - offs[None, :]
    valid = kidx >= 0
    kidx = jnp.maximum(kidx, 0)
    kg = jnp.take(k_ext, kidx, axis=1)
    vg = jnp.take(v_ext, kidx, axis=1)
    s = jnp.einsum('bqhd,bqkhd->bhqk', q, kg).astype(jnp.float32) * (HEAD_DIM ** -0.5)
    s = s - slopes[:, None, None] * offs.astype(jnp.float32)
    s = jnp.where(valid[None, None], s, -jnp.inf)
    lse = jax.nn.logsumexp(s, axis=-1)
    p = jnp.exp(s - lse[..., None]).astype(v_ext.dtype)
    o = jnp.einsum('bhqk,bqkhd->bqhd', p, vg)
    return o, lse


def combine_branches(outs, lses):
    w = jax.nn.softmax(jnp.stack(lses, axis=0), axis=0)
    acc = w[0].transpose(0, 2, 1)[..., None] * outs[0].astype(jnp.float32)
    for g in range(1, len(outs)):
        acc = acc + w[g].transpose(0, 2, 1)[..., None] * outs[g].astype(jnp.float32)
    return acc.astype(outs[0].dtype)


def dilated_prompt(q3, kv_sh, slopes):
    B, S = q3.shape[0], q3.shape[1]
    qs = [q3[:, :, g] for g in range(N_BRANCH)]
    ks = [kv_sh[:, :, g, 0] for g in range(N_BRANCH)]
    vs = [kv_sh[:, :, g, 1] for g in range(N_BRANCH)]

    def block(i):
        t0 = i * QBLK
        q_idx = t0 + jnp.arange(QBLK)
        outs, lses = [], []
        for g in range(N_BRANCH):
            qb = lax.dynamic_slice_in_dim(qs[g], t0, QBLK, axis=1)
            o, l = dilated_branch(qb, ks[g], vs[g], q_idx, DILATIONS[g], WINDOWS[g] // DILATIONS[g] + 1, slopes)
            outs.append(o)
            lses.append(l)
        return combine_branches(outs, lses)

    out = lax.map(block, jnp.arange(S // QBLK))
    return out.transpose(1, 0, 2, 3, 4).reshape(B, S, ATTN_WIDTH)


def dilated_sample(q3, k_exts, v_exts, q_idxs, slopes):
    DB, T = q3.shape[0], q3.shape[1]
    outs, lses = [], []
    for g in range(N_BRANCH):
        o, l = dilated_branch(q3[:, :, g], k_exts[g], v_exts[g], q_idxs[g], DILATIONS[g], WINDOWS[g] // DILATIONS[g] + 1, slopes)
        outs.append(o)
        lses.append(l)
    return combine_branches(outs, lses).reshape(DB, T, ATTN_WIDTH)


def setup_inputs(seed: int = 0) -> dict:
    key = jax.random.key(seed)
    ks = jax.random.split(key, 20)
    f32 = jnp.float32
    n_pages = PAST_LEN // PAGE_SIZE
    n_used = DEC_BATCH * n_pages
    n_pool = n_used + max(1, n_used // 4)
    page_table = jax.random.permutation(ks[0], n_pool)[:n_used].reshape(DEC_BATCH, n_pages).astype(jnp.int32)

    def nrm(k, shape, s=1.0):
        return s * jax.random.normal(k, shape, f32)

    x_prompt = nrm(ks[1], (BATCH, SEQ, D_MODEL))
    x_sample = nrm(ks[2], (DEC_BATCH, DEC_SEQ, D_MODEL))
    cache_fox_kv = nrm(ks[3], (N_A_LAYERS, n_pool, PAGE_SIZE, 2, N_HEADS, HEAD_DIM))
    cache_fox_logf = jax.nn.log_sigmoid(FORGET_BIAS_INIT + nrm(ks[4], (N_A_LAYERS, n_pool, PAGE_SIZE, N_HEADS)))
    cache_win0 = nrm(ks[5], (DEC_BATCH, min(WINDOWS[0], PAST_LEN), 2, N_HEADS, HEAD_DIM))
    cache_win1 = nrm(ks[6], (DEC_BATCH, min(WINDOWS[1], PAST_LEN), 2, N_HEADS, HEAD_DIM))
    cache_win2 = nrm(ks[7], (DEC_BATCH, min(WINDOWS[2], PAST_LEN), 2, N_HEADS, HEAD_DIM))
    norm_g = 1.0 + nrm(ks[8], (DEPTH, 2, D_MODEL), 0.02)
    w_qkvf = nrm(ks[9], (N_A_LAYERS, D_MODEL, 3 * ATTN_WIDTH + N_HEADS), D_MODEL ** -0.5)
    b_f = FORGET_BIAS_INIT + nrm(ks[10], (N_A_LAYERS, N_HEADS), 0.1)
    wo_a = nrm(ks[11], (N_A_LAYERS, ATTN_WIDTH, D_MODEL), ATTN_WIDTH ** -0.5)
    norm_kv = 1.0 + nrm(ks[12], (D_MODEL,), 0.02)
    w_kv_shared = nrm(ks[13], (D_MODEL, N_BRANCH * 2 * ATTN_WIDTH), D_MODEL ** -0.5)
    w_q_b = nrm(ks[14], (N_B_LAYERS, D_MODEL, N_BRANCH * ATTN_WIDTH), D_MODEL ** -0.5)
    wo_b = nrm(ks[15], (N_B_LAYERS, ATTN_WIDTH, D_MODEL), ATTN_WIDTH ** -0.5)
    w_gate_up = nrm(ks[16], (DEPTH, D_MODEL, 2 * D_FF), D_MODEL ** -0.5)
    w_down = nrm(ks[17], (DEPTH, D_FF, D_MODEL), D_FF ** -0.5)
    norm_final = 1.0 + nrm(ks[18], (D_MODEL,), 0.02)
    return {'x_prompt': x_prompt, 'x_sample': x_sample,
            'cache_fox_kv': cache_fox_kv, 'cache_fox_logf': cache_fox_logf,
            'cache_win0': cache_win0, 'cache_win1': cache_win1, 'cache_win2': cache_win2,
            'page_table': page_table,
            'norm_g': norm_g, 'w_qkvf': w_qkvf, 'b_f': b_f, 'wo_a': wo_a,
            'norm_kv': norm_kv, 'w_kv_shared': w_kv_shared, 'w_q_b': w_q_b, 'wo_b': wo_b,
            'w_gate_up': w_gate_up, 'w_down': w_down, 'norm_final': norm_final}


def reference(x_prompt, x_sample, cache_fox_kv, cache_fox_logf, cache_win0, cache_win1, cache_win2,
              page_table, norm_g, w_qkvf, b_f, wo_a, norm_kv, w_kv_shared, w_q_b, wo_b,
              w_gate_up, w_down, norm_final):
    slopes = alibi_slopes()
    B, S = x_prompt.shape[0], x_prompt.shape[1]
    DB, T = x_sample.shape[0], x_sample.shape[1]
    P = page_table.shape[1] * cache_fox_kv.shape[2]
    wins_in = (cache_win0, cache_win1, cache_win2)
    xp, xs = x_prompt, x_sample
    fox_kv_p, fox_lf_p, fox_kv_s, fox_lf_s = [], [], [], []
    for layer in range(DEPTH):
        hp = rmsnorm(xp, norm_g[layer, 0])
        hs = rmsnorm(xs, norm_g[layer, 0])
        if layer < N_A_LAYERS:
            a = layer
            qp, kp, vp, lfp = fox_project(hp, w_qkvf[a], b_f[a])
            qs, ks_, vs, lfs = fox_project(hs, w_qkvf[a], b_f[a])
            k_past = cache_fox_kv[a, page_table, :, 0].reshape(DB, P, N_HEADS, HEAD_DIM)
            v_past = cache_fox_kv[a, page_table, :, 1].reshape(DB, P, N_HEADS, HEAD_DIM)
            lf_past = cache_fox_logf[a, page_table].reshape(DB, P, N_HEADS)
            xp = xp + fox_prompt(qp, kp, vp, lfp) @ wo_a[a]
            xs = xs + fox_sample(qs, ks_, vs, lfs, k_past, v_past, lf_past) @ wo_a[a]
            fox_kv_p.append(jnp.stack([kp, vp], axis=2))
            fox_lf_p.append(lfp.astype(cache_fox_logf.dtype))
            fox_kv_s.append(jnp.stack([ks_, vs], axis=2))
            fox_lf_s.append(lfs.astype(cache_fox_logf.dtype))
        else:
            b = layer - N_A_LAYERS
            q3p = (hp @ w_q_b[b]).reshape(B, S, N_BRANCH, N_HEADS, HEAD_DIM)
            q3s = (hs @ w_q_b[b]).reshape(DB, T, N_BRANCH, N_HEADS, HEAD_DIM)
            xp = xp + dilated_prompt(q3p, kv_sh_p, slopes) @ wo_b[b]
            xs = xs + dilated_sample(q3s, k_exts, v_exts, q_idxs, slopes) @ wo_b[b]
        xp = xp + swiglu(rmsnorm(xp, norm_g[layer, 1]), w_gate_up[layer], w_down[layer])
        xs = xs + swiglu(rmsnorm(xs, norm_g[layer, 1]), w_gate_up[layer], w_down[layer])
        if layer == N_A_LAYERS - 1:
            kv_sh_p = (rmsnorm(xp, norm_kv) @ w_kv_shared).reshape(B, S, N_BRANCH, 2, N_HEADS, HEAD_DIM)
            kv_sh_s = (rmsnorm(xs, norm_kv) @ w_kv_shared).reshape(DB, T, N_BRANCH, 2, N_HEADS, HEAD_DIM)
            win_p, win_s, k_exts, v_exts, q_idxs = [], [], [], [], []
            for g in range(N_BRANCH):
                keep_p = min(WINDOWS[g], S)
                win_p.append(kv_sh_p[:, S - keep_p:, g])
                buf = wins_in[g]
                lc = buf.shape[1]
                ext = jnp.concatenate([buf, kv_sh_s[:, :, g].astype(buf.dtype)], axis=1)
                keep_s = min(WINDOWS[g], P + T)
                win_s.append(ext[:, ext.shape[1] - keep_s:])
                k_exts.append(ext[:, :, 0])
                v_exts.append(ext[:, :, 1])
                q_idxs.append(lc + jnp.arange(T))
    y_prompt = rmsnorm(xp, norm_final)
    y_sample = rmsnorm(xs, norm_final)
    return (y_prompt, y_sample, jnp.stack(fox_kv_p), jnp.stack(fox_lf_p), jnp.stack(fox_kv_s), jnp.stack(fox_lf_s),
            win_p[0], win_s[0], win_p[1], win_s[1], win_p[2], win_s[2])
```

```python
import functools

import jax
import jax.numpy as jnp
from jax import lax
from jax.experimental import pallas as pl
from jax.experimental.pallas import tpu as pltpu

F32 = jnp.float32
BF16 = jnp.bfloat16

N_HEADS = 16
HEAD_DIM = 64
N_BRANCH = 3
WINDOWS = (128, 512, 2048)
DILATIONS = (1, 4, 16)
N_TAPS = 129
RMS_EPS = 1e-6
NEG_BIG = -1e30
LANES = 128
HEADS_PER_BLOCK = LANES // HEAD_DIM
Q_SCALE = HEAD_DIM ** -0.5
VMEM_LIMIT = 56 * 1024 * 1024
FOX_TQ = 512

_NT = (((1,), (1,)), ((), ()))


def _params(*sem):
    return pltpu.CompilerParams(dimension_semantics=sem, vmem_limit_bytes=VMEM_LIMIT)


def _rmsnorm_bf16(x, g):
    y = x * lax.rsqrt(jnp.mean(x * x, axis=-1, keepdims=True) + RMS_EPS)
    return (y * g).astype(BF16)


def _log_sigmoid(x):
    return jnp.minimum(x, 0.0) - jnp.log1p(jnp.exp(-jnp.abs(x)))


def _select_head(q, hh):
    lane = lax.broadcasted_iota(jnp.int32, (1, LANES), 1)
    keep = ((lane // HEAD_DIM) == hh).astype(F32)
    return (q.astype(F32) * keep).astype(BF16)


def _norm_matmul_kernel(x_ref, g_ref, w_ref, o_ref, h_ref, *, scale):
    @pl.when(pl.program_id(1) == 0)
    def _():
        h_ref[...] = _rmsnorm_bf16(x_ref[...], g_ref[...])

    acc = jnp.dot(h_ref[...], w_ref[...], preferred_element_type=F32)
    if scale != 1.0:
        acc = acc * scale
    o_ref[...] = acc.astype(o_ref.dtype)


def norm_matmul(x, g, w, out_dtype, *, scale=1.0, tm=512, tn=512):
    rows, d = x.shape
    n = w.shape[1]
    tm = min(tm, rows)
    return pl.pallas_call(
        functools.partial(_norm_matmul_kernel, scale=scale),
        grid=(rows // tm, n // tn),
        in_specs=[
            pl.BlockSpec((tm, d), lambda i, j: (i, 0)),
            pl.BlockSpec((1, d), lambda i, j: (0, 0)),
            pl.BlockSpec((d, tn), lambda i, j: (0, j)),
        ],
        out_specs=pl.BlockSpec((tm, tn), lambda i, j: (i, j)),
        out_shape=jax.ShapeDtypeStruct((rows, n), out_dtype),
        scratch_shapes=[pltpu.VMEM((tm, d), BF16)],
        compiler_params=_params("parallel", "arbitrary"),
        name="norm_matmul",
    )(x, g.reshape(1, d), w)


def _fox_proj_kernel(x_ref, g_ref, w_ref, wf_ref, bf_ref, q_ref, kv_ref, kvb_ref, lf_ref):
    h = _rmsnorm_bf16(x_ref[...], g_ref[...])
    aw = q_ref.shape[1]
    q = jnp.dot(h, w_ref[:, 0:aw], preferred_element_type=F32)
    q_ref[...] = (q * Q_SCALE).astype(BF16)
    for c in range(2):
        kv = jnp.dot(h, w_ref[:, (c + 1) * aw:(c + 2) * aw], preferred_element_type=F32)
        kv_ref[:, c * aw:(c + 1) * aw] = kv
        kvb_ref[:, c * aw:(c + 1) * aw] = kv.astype(BF16)
    f = jnp.dot(h, wf_ref[...], preferred_element_type=F32)[:, 0:N_HEADS]
    lf_ref[...] = _log_sigmoid(f + bf_ref[...])


def fox_project(x, g, w_qkv, w_f, b_f, *, tm=512):
    rows, d = x.shape
    aw = w_qkv.shape[1] // 3
    tm = min(tm, rows)
    return pl.pallas_call(
        _fox_proj_kernel,
        grid=(rows // tm,),
        in_specs=[
            pl.BlockSpec((tm, d), lambda i: (i, 0)),
            pl.BlockSpec((1, d), lambda i: (0, 0)),
            pl.BlockSpec((d, 3 * aw), lambda i: (0, 0)),
            pl.BlockSpec((d, LANES), lambda i: (0, 0)),
            pl.BlockSpec((1, N_HEADS), lambda i: (0, 0)),
        ],
        out_specs=[
            pl.BlockSpec((tm, aw), lambda i: (i, 0)),
            pl.BlockSpec((tm, 2 * aw), lambda i: (i, 0)),
            pl.BlockSpec((tm, 2 * aw), lambda i: (i, 0)),
            pl.BlockSpec((tm, N_HEADS), lambda i: (i, 0)),
        ],
        out_shape=[
            jax.ShapeDtypeStruct((rows, aw), BF16),
            jax.ShapeDtypeStruct((rows, 2 * aw), F32),
            jax.ShapeDtypeStruct((rows, 2 * aw), BF16),
            jax.ShapeDtypeStruct((rows, N_HEADS), F32),
        ],
        compiler_params=_params("parallel"),
        name="fox_project",
    )(x, g.reshape(1, d), w_qkv, w_f, b_f.reshape(1, N_HEADS))


def _cumsum_kernel(lf_ref, c_ref, carry_ref):
    @pl.when(pl.program_id(1) == 0)
    def _():
        carry_ref[...] = jnp.zeros_like(carry_ref)

    ts = lf_ref.shape[1]
    src = lax.broadcasted_iota(jnp.int32, (ts, ts), 0)
    dst = lax.broadcasted_iota(jnp.int32, (ts, ts), 1)
    upper = (src <= dst).astype(F32)
    c = jnp.dot(lf_ref[...], upper, precision=lax.Precision.HIGHEST, preferred_element_type=F32)
    c = c + carry_ref[:, 0:1]
    c_ref[...] = c
    carry_ref[...] = jnp.broadcast_to(c[:, ts - 1:ts], carry_ref.shape)


def cumsum_seq(lf_t, *, ts=512):
    b, h, s = lf_t.shape
    return pl.pallas_call(
        _cumsum_kernel,
        grid=(b, s // ts),
        in_specs=[pl.BlockSpec((None, h, ts), lambda bi, i: (bi, 0, i))],
        out_specs=pl.BlockSpec((None, h, ts), lambda bi, i: (bi, 0, i)),
        out_shape=jax.ShapeDtypeStruct((b, h, s), F32),
        scratch_shapes=[pltpu.VMEM((h, LANES), F32)],
        compiler_params=_params("parallel", "arbitrary"),
        name="cumsum_seq",
    )(lf_t)


def _fox_prompt_kernel(q_ref, k_ref, v_ref, cq_ref, ck_ref, o_ref, *, tq):
    i = pl.program_id(2)
    q = q_ref[...]
    lane = lax.broadcasted_iota(jnp.int32, (1, LANES), 1)
    qm = [_select_head(q, hh) for hh in range(HEADS_PER_BLOCK)]
    cq = [cq_ref[:, hh:hh + 1] for hh in range(HEADS_PER_BLOCK)]

    def tile(j, carry, masked):
        start = pl.multiple_of(j * tq, tq)
        k = k_ref[pl.ds(start, tq), :]
        v = v_ref[pl.ds(start, tq), :]
        new = []
        for hh in range(HEADS_PER_BLOCK):
            m, l, acc = carry[hh]
            s = lax.dot_general(qm[hh], k, _NT, preferred_element_type=F32)
            s = s + (cq[hh] - ck_ref[j, hh:hh + 1, :])
            if masked:
                row = lax.broadcasted_iota(jnp.int32, (tq, tq), 0)
                col = lax.broadcasted_iota(jnp.int32, (tq, tq), 1)
                s = jnp.where(col <= row, s, NEG_BIG)
            m_new = jnp.maximum(m, jnp.max(s, axis=-1, keepdims=True))
            alpha = jnp.exp(m - m_new)
            p = jnp.exp(s - m_new)
            l = alpha * l + jnp.sum(p, axis=-1, keepdims=True)
            acc = alpha * acc + jnp.dot(p.astype(BF16), v, preferred_element_type=F32)
            new.append((m_new, l, acc))
        return tuple(new)

    init = tuple(
        (jnp.full((tq, 1), NEG_BIG, F32), jnp.zeros((tq, 1), F32), jnp.zeros((tq, LANES), F32))
        for _ in range(HEADS_PER_BLOCK))
    carry = lax.fori_loop(0, i, lambda j, c: tile(j, c, False), init)
    carry = tile(i, carry, True)
    outs = [acc / l for (_, l, acc) in carry]
    o_ref[...] = jnp.where(lane < HEAD_DIM, outs[0], outs[1]).astype(o_ref.dtype)


def fox_prompt_attention(q, kvb, c_cols, c_rows, *, tq=512):
    b, s, aw = q.shape
    nblk = aw // LANES
    assert c_rows.shape[2:] == (s // tq, HEADS_PER_BLOCK, tq)
    return pl.pallas_call(
        functools.partial(_fox_prompt_kernel, tq=tq),
        grid=(b, nblk, s // tq),
        in_specs=[
            pl.BlockSpec((None, tq, LANES), lambda bi, hp, i: (bi, i, hp)),
            pl.BlockSpec((None, s, LANES), lambda bi, hp, i: (bi, 0, hp)),
            pl.BlockSpec((None, s, LANES), lambda bi, hp, i: (bi, 0, nblk + hp)),
            pl.BlockSpec((None, None, tq, HEADS_PER_BLOCK), lambda bi, hp, i: (bi, hp, i, 0)),
            pl.BlockSpec((None, None, s // tq, HEADS_PER_BLOCK, tq), lambda bi, hp, i: (bi, hp, 0, 0, 0)),
        ],
        out_specs=pl.BlockSpec((None, tq, LANES), lambda bi, hp, i: (bi, i, hp)),
        out_shape=jax.ShapeDtypeStruct((b, s, aw), BF16),
        compiler_params=_params("parallel", "parallel", "arbitrary"),
        name="fox_prompt_attention",
    )(q, kvb, kvb, c_cols, c_rows)


def _head_mask(rows):
    head = lax.broadcasted_iota(jnp.int32, (rows, N_HEADS * HEAD_DIM), 1) // HEAD_DIM
    return head == lax.broadcasted_iota(jnp.int32, (rows, N_HEADS * HEAD_DIM), 0)


def _row_to_col(row):
    eye = (lax.broadcasted_iota(jnp.int32, (N_HEADS, N_HEADS), 0)
           == lax.broadcasted_iota(jnp.int32, (N_HEADS, N_HEADS), 1))
    return jnp.sum(jnp.where(eye, jnp.broadcast_to(row, (N_HEADS, N_HEADS)), 0.0), axis=1, keepdims=True)


def _fox_sample_kernel(pt_ref, q_ref, kvn_ref, lfn_ref, kvp_ref, lfp_ref, o_ref,
                       qbd_ref, m_ref, l_ref, acc_ref, carry_ref, *, n_new):
    del pt_ref
    p = pl.program_id(1)
    aw = N_HEADS * HEAD_DIM
    nrow = n_new * N_HEADS
    mask16 = _head_mask(N_HEADS)

    @pl.when(p == 0)
    def _():
        for t in range(n_new):
            qt = jnp.broadcast_to(q_ref[t:t + 1, :], (N_HEADS, aw))
            qbd_ref[t * N_HEADS:(t + 1) * N_HEADS, :] = jnp.where(mask16, qt, 0.0)
        m_ref[...] = jnp.full_like(m_ref, NEG_BIG)
        l_ref[...] = jnp.zeros_like(l_ref)
        acc_ref[...] = jnp.zeros_like(acc_ref)
        carry_ref[...] = jnp.zeros_like(carry_ref)

    page = lfp_ref.shape[1]
    lf_t = lfp_ref[...]
    src = lax.broadcasted_iota(jnp.int32, (page, page), 0)
    dst = lax.broadcasted_iota(jnp.int32, (page, page), 1)
    later = (src > dst).astype(F32)
    suffix = jnp.dot(lf_t, later, precision=lax.Precision.HIGHEST, preferred_element_type=F32)
    suffix = suffix + carry_ref[:, 0:1]
    carry_ref[...] = jnp.broadcast_to(suffix[:, 0:1] + lf_t[:, 0:1], carry_ref.shape)

    cn_cols = []
    run = jnp.zeros((1, N_HEADS), F32)
    for t in range(n_new):
        run = run + lfn_ref[t:t + 1, :]
        cn_cols.append(jnp.broadcast_to(_row_to_col(run), (N_HEADS, LANES)))
    cn = jnp.concatenate(cn_cols, axis=0)[:, 0:1]

    qbd = qbd_ref[...]
    qbd_b = qbd.astype(BF16)
    k = kvp_ref[:, 0:aw].astype(BF16)
    v = kvp_ref[:, aw:2 * aw].astype(BF16)
    s = lax.dot_general(qbd_b, k, _NT, preferred_element_type=F32)
    s = s + jnp.concatenate([suffix + c for c in cn_cols], axis=0)
    m_old = m_ref[:, 0:1]
    m_new = jnp.maximum(m_old, jnp.max(s, axis=-1, keepdims=True))
    alpha = jnp.exp(m_old - m_new)
    pr = jnp.exp(s - m_new)
    l_new = alpha * l_ref[:, 0:1] + jnp.sum(pr, axis=-1, keepdims=True)
    acc_new = alpha * acc_ref[...] + jnp.dot(pr.astype(BF16), v, preferred_element_type=F32)
    m_ref[...] = jnp.broadcast_to(m_new, m_ref.shape)
    l_ref[...] = jnp.broadcast_to(l_new, l_ref.shape)
    acc_ref[...] = acc_new

    @pl.when(p == pl.num_programs(1) - 1)
    def _():
        tok = lax.broadcasted_iota(jnp.int32, (nrow, LANES), 0)[:, 0:1] // N_HEADS
        s_new = []
        for t2 in range(n_new):
            kt = kvn_ref[t2:t2 + 1, 0:aw].astype(BF16).astype(F32)
            st = jnp.sum(qbd * kt, axis=-1, keepdims=True)
            st = st + cn - jnp.concatenate([cn_cols[t2]] * n_new, axis=0)[:, 0:1]
            s_new.append(jnp.where(tok >= t2, st, NEG_BIG))
        m_fin = m_new
        for st in s_new:
            m_fin = jnp.maximum(m_fin, st)
        beta = jnp.exp(m_new - m_fin)
        l_fin = beta * l_new
        acc_fin = beta * acc_new
        for t2 in range(n_new):
            pt = jnp.exp(s_new[t2] - m_fin)
            l_fin = l_fin + pt
            vt = kvn_ref[t2:t2 + 1, aw:2 * aw].astype(BF16).astype(F32)
            acc_fin = acc_fin + pt.astype(BF16).astype(F32) * vt
        out = acc_fin / l_fin
        for t in range(n_new):
            blk = jnp.where(mask16, out[t * N_HEADS:(t + 1) * N_HEADS, :], 0.0)
            o_ref[t:t + 1, :] = jnp.sum(blk, axis=0, keepdims=True)


def fox_sample_attention(page_table, q, kv_new, lf_new, cache_kv, cache_lf_t, layer):
    db, t, aw = q.shape
    npg = page_table.shape[0] // db
    page = cache_kv.shape[2]
    nrow = t * N_HEADS

    def page_idx(b, p, pt):
        return pt[b * npg + (npg - 1 - p)]

    grid_spec = pltpu.PrefetchScalarGridSpec(
        num_scalar_prefetch=1,
        grid=(db, npg),
        in_specs=[
            pl.BlockSpec((None, t, aw), lambda b, p, pt: (b, 0, 0)),
            pl.BlockSpec((None, t, 2 * aw), lambda b, p, pt: (b, 0, 0)),
            pl.BlockSpec((None, t, N_HEADS), lambda b, p, pt: (b, 0, 0)),
            pl.BlockSpec((None, None, page, 2 * aw), lambda b, p, pt: (layer, page_idx(b, p, pt), 0, 0)),
            pl.BlockSpec((None, None, N_HEADS, page), lambda b, p, pt: (layer, page_idx(b, p, pt), 0, 0)),
        ],
        out_specs=pl.BlockSpec((None, t, aw), lambda b, p, pt: (b, 0, 0)),
        scratch_shapes=[
            pltpu.VMEM((nrow, aw), F32),
            pltpu.VMEM((nrow, LANES), F32),
            pltpu.VMEM((nrow, LANES), F32),
            pltpu.VMEM((nrow, aw), F32),
            pltpu.VMEM((N_HEADS, LANES), F32),
        ],
    )
    return pl.pallas_call(
        functools.partial(_fox_sample_kernel, n_new=t),
        grid_spec=grid_spec,
        out_shape=jax.ShapeDtypeStruct((db, t, aw), F32),
        compiler_params=_params("parallel", "arbitrary"),
        name="fox_sample_attention",
    )(page_table, q, kv_new, lf_new, cache_kv, cache_lf_t)


def _dilated_prompt_kernel(slope_ref, q_ref, kp_ref, vp_ref, kc_ref, vc_ref, o_ref, lse_ref):
    u = pl.program_id(2)
    tu = q_ref.shape[0]
    lane = lax.broadcasted_iota(jnp.int32, (1, LANES), 1)
    iq = lax.broadcasted_iota(jnp.int32, (tu, tu), 0)
    ik = lax.broadcasted_iota(jnp.int32, (tu, tu), 1)
    dist_c = (iq - ik).astype(F32)
    dist_p = (iq - ik + tu).astype(F32)
    ok_c = ik <= iq
    ok_p = jnp.logical_and(ik >= iq, u > 0)
    for hp in range(N_HEADS // HEADS_PER_BLOCK):
        sl = slice(hp * LANES, (hp + 1) * LANES)
        q = q_ref[:, sl]
        kp, vp, kc, vc = kp_ref[:, sl], vp_ref[:, sl], kc_ref[:, sl], vc_ref[:, sl]
        outs, lses = [], []
        for hh in range(HEADS_PER_BLOCK):
            slope = slope_ref[hp * HEADS_PER_BLOCK + hh]
            qm = _select_head(q, hh)
            s_c = lax.dot_general(qm, kc, _NT, preferred_element_type=F32)
            s_p = lax.dot_general(qm, kp, _NT, preferred_element_type=F32)
            s_c = jnp.where(ok_c, s_c - slope * dist_c, NEG_BIG)
            s_p = jnp.where(ok_p, s_p - slope * dist_p, NEG_BIG)
            m = jnp.maximum(jnp.max(s_c, axis=-1, keepdims=True), jnp.max(s_p, axis=-1, keepdims=True))
            p_c = jnp.exp(s_c - m)
            p_p = jnp.exp(s_p - m)
            l = jnp.sum(p_c, axis=-1, keepdims=True) + jnp.sum(p_p, axis=-1, keepdims=True)
            acc = (jnp.dot(p_c.astype(BF16), vc, preferred_element_type=F32)
                   + jnp.dot(p_p.astype(BF16), vp, preferred_element_type=F32))
            outs.append(acc / l)
            lses.append(m + jnp.log(l))
        o_ref[:, sl] = jnp.where(lane < HEAD_DIM, outs[0], outs[1])
        lse_ref[:, sl] = jnp.where(lane < HEAD_DIM, lses[0], lses[1])


def dilated_prompt_branch(q3, kvsh, slopes_d, g, *, tu=128):
    d = DILATIONS[g]
    b, s, _ = q3.shape
    aw = N_HEADS * HEAD_DIM
    su = s // d
    qv = q3.reshape(b, su, d * N_BRANCH * aw)
    kvv = kvsh.reshape(b, su, d * N_BRANCH * 2 * aw)

    def prev(u):
        return jnp.maximum(u - 1, 0)

    blk = (None, tu, aw)
    outs = pl.pallas_call(
        _dilated_prompt_kernel,
        grid=(b, d, su // tu),
        in_specs=[
            pl.BlockSpec(memory_space=pltpu.SMEM),
            pl.BlockSpec(blk, lambda bi, r, u: (bi, u, N_BRANCH * r + g)),
            pl.BlockSpec(blk, lambda bi, r, u: (bi, prev(u), 2 * N_BRANCH * r + 2 * g)),
            pl.BlockSpec(blk, lambda bi, r, u: (bi, prev(u), 2 * N_BRANCH * r + 2 * g + 1)),
            pl.BlockSpec(blk, lambda bi, r, u: (bi, u, 2 * N_BRANCH * r + 2 * g)),
            pl.BlockSpec(blk, lambda bi, r, u: (bi, u, 2 * N_BRANCH * r + 2 * g + 1)),
        ],
        out_specs=[
            pl.BlockSpec(blk, lambda bi, r, u: (bi, u, r)),
            pl.BlockSpec(blk, lambda bi, r, u: (bi, u, r)),
        ],
        out_shape=[
            jax.ShapeDtypeStruct((b, su, d * aw), F32),
            jax.ShapeDtypeStruct((b, su, d * aw), F32),
        ],
        compiler_params=_params("parallel", "parallel", "arbitrary"),
        name=f"dilated_prompt_branch{g}",
    )(slopes_d, qv, kvv, kvv, kvv, kvv)
    return outs[0].reshape(b * s, aw), outs[1].reshape(b * s, aw)


def _dilated_sample_kernel(slope_ref, q_ref, kvn_ref, c0_ref, c1_ref, c2_ref, o_ref, *, n_new):
    aw = N_HEADS * HEAD_DIM
    mask16 = _head_mask(N_HEADS)
    caches = (c0_ref, c1_ref, c2_ref)
    rows = c0_ref.shape[0]
    row_i = lax.broadcasted_iota(jnp.int32, (1, rows), 1)
    slope = slope_ref[...]
    for t in range(n_new):
        scores, values, news = [], [], []
        for g in range(N_BRANCH):
            d = DILATIONS[g]
            q = q_ref[t:t + 1, g * aw:(g + 1) * aw]
            qbd = jnp.where(mask16, jnp.broadcast_to(q, (N_HEADS, aw)), 0.0)
            col = 0 if g == 0 else t * 2 * aw
            k = caches[g][:, col:col + aw].astype(BF16)
            v = caches[g][:, col + aw:col + 2 * aw].astype(BF16)
            s = lax.dot_general(qbd.astype(BF16), k, _NT, preferred_element_type=F32)
            if g == 0:
                taps = (rows + t - row_i).astype(F32)
                s = jnp.where(row_i >= t, s - slope * taps, NEG_BIG)
            else:
                taps = ((rows - row_i) * d).astype(F32)
                s = s - slope * taps
            scores.append(s)
            values.append(v)
            for t2 in (range(t + 1) if g == 0 else (t,)):
                base = g * 2 * aw
                kn = kvn_ref[t2:t2 + 1, base:base + aw].astype(BF16).astype(F32)
                vn = kvn_ref[t2:t2 + 1, base + aw:base + 2 * aw].astype(BF16).astype(F32)
                sn = jnp.sum(qbd * kn, axis=-1, keepdims=True) - slope * float((t - t2) * d)
                news.append((sn, vn))
        m = scores[0].max(axis=-1, keepdims=True)
        for s in scores[1:]:
            m = jnp.maximum(m, s.max(axis=-1, keepdims=True))
        for sn, _ in news:
            m = jnp.maximum(m, sn)
        l = jnp.zeros((N_HEADS, 1), F32)
        acc = jnp.zeros((N_HEADS, aw), F32)
        for s, v in zip(scores, values):
            p = jnp.exp(s - m)
            l = l + jnp.sum(p, axis=-1, keepdims=True)
            acc = acc + jnp.dot(p.astype(BF16), v, preferred_element_type=F32)
        for sn, vn in news:
            p = jnp.exp(sn - m)
            l = l + p
            acc = acc + p.astype(BF16).astype(F32) * vn
        out = jnp.where(mask16, acc / l, 0.0)
        o_ref[t:t + 1, :] = jnp.sum(out, axis=0, keepdims=True)


def dilated_sample_attention(q3, kv_new, caches, slopes):
    db, t, _ = q3.shape
    aw = N_HEADS * HEAD_DIM
    rows = N_TAPS - 1
    views = [caches[g].reshape(db, rows, DILATIONS[g] * 2 * aw) for g in range(N_BRANCH)]
    widths = [2 * aw, t * 2 * aw, t * 2 * aw]
    return pl.pallas_call(
        functools.partial(_dilated_sample_kernel, n_new=t),
        grid=(db,),
        in_specs=[
            pl.BlockSpec((N_HEADS, 1), lambda b: (0, 0)),
            pl.BlockSpec((None, t, N_BRANCH * aw), lambda b: (b, 0, 0)),
            pl.BlockSpec((None, t, N_BRANCH * 2 * aw), lambda b: (b, 0, 0)),
        ] + [pl.BlockSpec((None, rows, widths[g]), lambda b: (b, 0, 0)) for g in range(N_BRANCH)],
        out_specs=pl.BlockSpec((None, t, aw), lambda b: (b, 0, 0)),
        out_shape=jax.ShapeDtypeStruct((db, t, aw), F32),
        compiler_params=_params("parallel"),
        name="dilated_sample_attention",
    )(slopes.reshape(N_HEADS, 1), q3, kv_new, *views)


def _proj_residual_kernel(x_ref, o_ref, w_ref, y_ref):
    y_ref[...] = x_ref[...] + jnp.dot(o_ref[...].astype(BF16), w_ref[...], preferred_element_type=F32)


def proj_residual(x, o, w, *, tm=512):
    rows, d = x.shape
    aw = o.shape[1]
    tm = min(tm, rows)
    return pl.pallas_call(
        _proj_residual_kernel,
        grid=(rows // tm,),
        in_specs=[
            pl.BlockSpec((tm, d), lambda i: (i, 0)),
            pl.BlockSpec((tm, aw), lambda i: (i, 0)),
            pl.BlockSpec((aw, d), lambda i: (0, 0)),
        ],
        out_specs=pl.BlockSpec((tm, d), lambda i: (i, 0)),
        out_shape=jax.ShapeDtypeStruct((rows, d), F32),
        compiler_params=_params("parallel"),
        name="proj_residual",
    )(x, o, w)


def _merge_proj_residual_kernel(x_ref, o0_ref, o1_ref, o2_ref, l0_ref, l1_ref, l2_ref, w_ref, y_ref):
    l0, l1, l2 = l0_ref[...], l1_ref[...], l2_ref[...]
    m = jnp.maximum(jnp.maximum(l0, l1), l2)
    e0, e1, e2 = jnp.exp(l0 - m), jnp.exp(l1 - m), jnp.exp(l2 - m)
    inv = 1.0 / (e0 + e1 + e2)
    o = (e0 * inv) * o0_ref[...] + (e1 * inv) * o1_ref[...] + (e2 * inv) * o2_ref[...]
    y_ref[...] = x_ref[...] + jnp.dot(o.astype(BF16), w_ref[...], preferred_element_type=F32)


def merge_proj_residual(x, outs, lses, w, *, tm=256):
    rows, d = x.shape
    aw = w.shape[0]
    tm = min(tm, rows)
    row_spec = pl.BlockSpec((tm, aw), lambda i: (i, 0))
    return pl.pallas_call(
        _merge_proj_residual_kernel,
        grid=(rows // tm,),
        in_specs=[pl.BlockSpec((tm, d), lambda i: (i, 0))] + [row_spec] * 6
        + [pl.BlockSpec((aw, d), lambda i: (0, 0))],
        out_specs=pl.BlockSpec((tm, d), lambda i: (i, 0)),
        out_shape=jax.ShapeDtypeStruct((rows, d), F32),
        compiler_params=_params("parallel"),
        name="merge_proj_residual",
    )(x, *outs, *lses, w)


def _ffn_kernel(x_ref, g_ref, wg_ref, wu_ref, wd_ref, y_ref, h_ref, acc_ref):
    j = pl.program_id(1)

    @pl.when(j == 0)
    def _():
        h_ref[...] = _rmsnorm_bf16(x_ref[...], g_ref[...])
        acc_ref[...] = jnp.zeros_like(acc_ref)

    h = h_ref[...]
    gate = jnp.dot(h, wg_ref[...], preferred_element_type=F32)
    up = jnp.dot(h, wu_ref[...], preferred_element_type=F32)
    act = gate * (1.0 / (1.0 + jnp.exp(-gate))) * up
    acc_ref[...] += jnp.dot(act.astype(BF16), wd_ref[...], preferred_element_type=F32)

    @pl.when(j == pl.num_programs(1) - 1)
    def _():
        y_ref[...] = x_ref[...] + acc_ref[...]


def ffn_residual(x, g, w_gate_up, w_down, *, tm=1024, tf=256):
    rows, d = x.shape
    f = w_down.shape[0]
    tm = min(tm, rows)
    nf = f // tf
    return pl.pallas_call(
        _ffn_kernel,
        grid=(rows // tm, nf),
        in_specs=[
            pl.BlockSpec((tm, d), lambda i, j: (i, 0)),
            pl.BlockSpec((1, d), lambda i, j: (0, 0)),
            pl.BlockSpec((d, tf), lambda i, j: (0, j)),
            pl.BlockSpec((d, tf), lambda i, j: (0, nf + j)),
            pl.BlockSpec((tf, d), lambda i, j: (j, 0)),
        ],
        out_specs=pl.BlockSpec((tm, d), lambda i, j: (i, 0)),
        out_shape=jax.ShapeDtypeStruct((rows, d), F32),
        scratch_shapes=[pltpu.VMEM((tm, d), BF16), pltpu.VMEM((tm, d), F32)],
        compiler_params=_params("parallel", "arbitrary"),
        name="ffn_residual",
    )(x, g.reshape(1, d), w_gate_up, w_gate_up, w_down)


def _rmsnorm_kernel(x_ref, g_ref, y_ref):
    x = x_ref[...]
    y = x * lax.rsqrt(jnp.mean(x * x, axis=-1, keepdims=True) + RMS_EPS)
    y_ref[...] = y * g_ref[...]


def rmsnorm(x, g, *, tm=512):
    rows, d = x.shape
    tm = min(tm, rows)
    return pl.pallas_call(
        _rmsnorm_kernel,
        grid=(rows // tm,),
        in_specs=[pl.BlockSpec((tm, d), lambda i: (i, 0)), pl.BlockSpec((1, d), lambda i: (0, 0))],
        out_specs=pl.BlockSpec((tm, d), lambda i: (i, 0)),
        out_shape=jax.ShapeDtypeStruct((rows, d), F32),
        compiler_params=_params("parallel"),
        name="rmsnorm",
    )(x, g.reshape(1, d))


def kernel(x_prompt, x_sample, cache_fox_kv, cache_fox_logf, cache_win0, cache_win1, cache_win2, page_table,
           norm_g, w_qkvf, b_f, wo_a, norm_kv, w_kv_shared, w_q_b, wo_b, w_gate_up, w_down, norm_final):
    b, s, d = x_prompt.shape
    db, t, _ = x_sample.shape
    aw = N_HEADS * HEAD_DIM
    n_a = w_qkvf.shape[0]
    depth = norm_g.shape[0]
    n_pool, page = cache_fox_kv.shape[1], cache_fox_kv.shape[2]
    caches = (cache_win0, cache_win1, cache_win2)
    assert d == aw and s % (DILATIONS[-1] * 128) == 0 and t <= DILATIONS[1]
    assert all(c.shape[1] == w for c, w in zip(caches, WINDOWS)) and WINDOWS[-1] <= s

    slopes = jnp.exp2(-8.0 * jnp.arange(1, N_HEADS + 1, dtype=F32) / N_HEADS)
    cache_kv = cache_fox_kv.reshape(n_a, n_pool, page, 2 * aw)
    cache_lf_t = cache_fox_logf.transpose(0, 1, 3, 2)
    pt_flat = page_table.reshape(-1)
    cache_views = [c.reshape(db, c.shape[1], 2 * aw) for c in caches]

    xp = x_prompt.reshape(b * s, d)
    xs = x_sample.reshape(db * t, d)
    fox_kv_p, fox_lf_p, fox_kv_s, fox_lf_s = [], [], [], []
    for layer in range(depth):
        g1, g2 = norm_g[layer, 0], norm_g[layer, 1]
        if layer < n_a:
            a = layer
            w_qkv = w_qkvf[a, :, :3 * aw].astype(BF16)
            w_f = jnp.pad(w_qkvf[a, :, 3 * aw:], ((0, 0), (0, LANES - N_HEADS))).astype(BF16)
            wo = wo_a[a].astype(BF16)
            q, kv, kvb, lf = fox_project(xp, g1, w_qkv, w_f, b_f[a])
            c = cumsum_seq(lf.reshape(b, s, N_HEADS).transpose(0, 2, 1))
            c = c.reshape(b, N_HEADS // HEADS_PER_BLOCK, HEADS_PER_BLOCK, s)
            c_cols = c.transpose(0, 1, 3, 2)
            c_rows = c.reshape(c.shape[:3] + (s // FOX_TQ, FOX_TQ)).transpose(0, 1, 3, 2, 4)
            o = fox_prompt_attention(q.reshape(b, s, aw), kvb.reshape(b, s, 2 * aw), c_cols, c_rows, tq=FOX_TQ)
            xp = proj_residual(xp, o.reshape(b * s, aw), wo)
            fox_kv_p.append(kv.reshape(b, s, 2, N_HEADS, HEAD_DIM))
            fox_lf_p.append(lf.reshape(b, s, N_HEADS))
            q, kv, _, lf = fox_project(xs, g1, w_qkv, w_f, b_f[a])
            o = fox_sample_attention(pt_flat, q.astype(F32).reshape(db, t, aw), kv.reshape(db, t, 2 * aw),
                                     lf.reshape(db, t, N_HEADS), cache_kv, cache_lf_t, a)
            xs = proj_residual(xs, o.reshape(db * t, aw), wo)
            fox_kv_s.append(kv.reshape(db, t, 2, N_HEADS, HEAD_DIM))
            fox_lf_s.append(lf.reshape(db, t, N_HEADS))
        else:
            bl = layer - n_a
            wq = w_q_b[bl].astype(BF16)
            wo = wo_b[bl].astype(BF16)
            q3 = norm_matmul(xp, g1, wq, BF16, scale=Q_SCALE).reshape(b, s, N_BRANCH * aw)
            outs, lses = [], []
            for g in range(N_BRANCH):
                o_g, lse_g = dilated_prompt_branch(q3, kvsh_p, slopes * float(DILATIONS[g]), g)
                outs.append(o_g)
                lses.append(lse_g)
            xp = merge_proj_residual(xp, outs, lses, wo)
            q3 = norm_matmul(xs, g1, wq, BF16, scale=Q_SCALE).astype(F32).reshape(db, t, N_BRANCH * aw)
            o = dilated_sample_attention(q3, kvsh_s, cache_views, slopes)
            xs = proj_residual(xs, o.reshape(db * t, aw), wo)
        wgu = w_gate_up[layer].astype(BF16)
        wd = w_down[layer].astype(BF16)
        xp = ffn_residual(xp, g2, wgu, wd)
        xs = ffn_residual(xs, g2, wgu, wd)
        if layer == n_a - 1:
            wkv = w_kv_shared.astype(BF16)
            kvsh_p = norm_matmul(xp, norm_kv, wkv, BF16).reshape(b, s, N_BRANCH * 2 * aw)
            tail = WINDOWS[-1]
            x_tail = xp.reshape(b, s, d)[:, s - tail:].reshape(b * tail, d)
            kv_tail = norm_matmul(x_tail, norm_kv, wkv, F32).reshape(b, tail, N_BRANCH, 2, N_HEADS, HEAD_DIM)
            kvsh_s = norm_matmul(xs, norm_kv, wkv, F32).reshape(db, t, N_BRANCH * 2 * aw)
            kv_new = kvsh_s.reshape(db, t, N_BRANCH, 2, N_HEADS, HEAD_DIM)
            win_p = [kv_tail[:, tail - WINDOWS[g]:, g] for g in range(N_BRANCH)]
            win_s = [jnp.concatenate([caches[g][:, t:], kv_new[:, :, g]], axis=1) for g in range(N_BRANCH)]
    y_prompt = rmsnorm(xp, norm_final).reshape(b, s, d)
    y_sample = rmsnorm(xs, norm_final).reshape(db, t, d)
    return (y_prompt, y_sample, jnp.stack(fox_kv_p), jnp.stack(fox_lf_p), jnp.stack(fox_kv_s), jnp.stack(fox_lf_s),
            win_p[0], win_s[0], win_p[1], win_s[1], win_p[2], win_s[2])
```

```python
import functools

import numpy as np
import jax
import jax.numpy as jnp
from jax import lax
from jax.experimental import pallas as pl
from jax.experimental.pallas import tpu as pltpu

F32 = jnp.float32
BF16 = jnp.bfloat16

N_HEADS = 16
HEAD_DIM = 64
AW = N_HEADS * HEAD_DIM
N_BRANCH = 3
WINDOWS = (128, 512, 2048)
DILATIONS = (1, 4, 16)
N_TAPS = 129
RMS_EPS = 1e-6
NEG_BIG = -1e30
LANES = 128
HEADS_PER_BLOCK = LANES // HEAD_DIM
N_PAIRS = N_HEADS // HEADS_PER_BLOCK
Q_SCALE = HEAD_DIM ** -0.5
VMEM_LIMIT = 56 * 1024 * 1024
ROW_TILE = 512
FOX_SUB = 512
DIL_TILE = 128
FOX_PAGES_PER_STEP = 4

_NT = (((1,), (1,)), ((), ()))


def _params(*sem):
    return pltpu.CompilerParams(dimension_semantics=sem, vmem_limit_bytes=VMEM_LIMIT)


def _rmsnorm_bf16(x, g):
    y = x * lax.rsqrt(jnp.mean(x * x, axis=-1, keepdims=True) + RMS_EPS)
    return (y * g).astype(BF16)


def _log_sigmoid(x):
    return jnp.minimum(x, 0.0) - jnp.log1p(jnp.exp(-jnp.abs(x)))


def _select_head(q, hh):
    lane = lax.broadcasted_iota(jnp.int32, (1, LANES), 1)
    keep = ((lane // HEAD_DIM) == hh).astype(F32)
    return (q.astype(F32) * keep).astype(BF16)


def _dot_nt(a, b):
    return lax.dot_general(a, b, _NT, preferred_element_type=F32)


def _deinterleave_matrix(n, d):
    p = np.zeros((n, n), np.float32)
    i = np.arange(n)
    p[(i % d) * (n // d) + i // d, i] = 1.0
    return p


def _norm_matmul_kernel(x_ref, g_ref, w_ref, o_ref, h_ref, *, scale):
    @pl.when(pl.program_id(1) == 0)
    def _():
        h_ref[...] = _rmsnorm_bf16(x_ref[...], g_ref[...])

    acc = jnp.dot(h_ref[...], w_ref[...], preferred_element_type=F32)
    o_ref[...] = (acc * scale).astype(o_ref.dtype)


def norm_matmul(x, g, w, out_dtype, *, scale=1.0, tn=512):
    rows, d = x.shape
    n = w.shape[1]
    tm = min(ROW_TILE, rows)
    return pl.pallas_call(
        functools.partial(_norm_matmul_kernel, scale=scale),
        grid=(rows // tm, n // tn),
        in_specs=[
            pl.BlockSpec((tm, d), lambda i, j: (i, 0)),
            pl.BlockSpec((1, d), lambda i, j: (0, 0)),
            pl.BlockSpec((d, tn), lambda i, j: (0, j)),
        ],
        out_specs=pl.BlockSpec((tm, tn), lambda i, j: (i, j)),
        out_shape=jax.ShapeDtypeStruct((rows, n), out_dtype),
        scratch_shapes=[pltpu.VMEM((tm, d), BF16)],
        compiler_params=_params("parallel", "arbitrary"),
        name="norm_matmul",
    )(x, g.reshape(1, d), w)


def _norm_matmul_grouped_kernel(x_ref, g_ref, p_ref, w_ref, o_ref, h_ref, *, scale, dil):
    @pl.when(pl.program_id(2) == 0)
    def _():
        h = _rmsnorm_bf16(x_ref[...], g_ref[...])
        if dil > 1:
            h = jnp.dot(p_ref[...], h, preferred_element_type=F32).astype(BF16)
        h_ref[...] = h

    acc = jnp.dot(h_ref[...], w_ref[...], preferred_element_type=F32) * scale
    o_ref[...] = acc.reshape(o_ref.shape).astype(o_ref.dtype)


def norm_matmul_grouped(x, g, w, dil, *, scale=1.0, tn=512):
    b, s, d = x.shape
    n = w.shape[1]
    tm = ROW_TILE
    perm = jnp.asarray(_deinterleave_matrix(tm, dil), BF16)
    return pl.pallas_call(
        functools.partial(_norm_matmul_grouped_kernel, scale=scale, dil=dil),
        grid=(b, s // tm, n // tn),
        in_specs=[
            pl.BlockSpec((None, tm, d), lambda bi, i, j: (bi, i, 0)),
            pl.BlockSpec((1, d), lambda bi, i, j: (0, 0)),
            pl.BlockSpec((tm, tm), lambda bi, i, j: (0, 0)),
            pl.BlockSpec((d, tn), lambda bi, i, j: (0, j)),
        ],
        out_specs=pl.BlockSpec((None, dil, tm // dil, tn), lambda bi, i, j: (bi, 0, i, j)),
        out_shape=jax.ShapeDtypeStruct((b, dil, s // dil, n), BF16),
        scratch_shapes=[pltpu.VMEM((tm, d), BF16)],
        compiler_params=_params("parallel", "parallel", "arbitrary"),
        name=f"norm_matmul_grouped{dil}",
    )(x, g.reshape(1, d), perm, w)


def _fox_proj_prompt_kernel(x_ref, g_ref, wt_ref, wft_ref, bf_ref, kv_in_ref, q_ref, kv_ref, kvb_ref, lf_ref):
    del kv_in_ref
    h = _rmsnorm_bf16(x_ref[...], g_ref[...])
    q_ref[...] = (_dot_nt(h, wt_ref[0:AW, :]) * Q_SCALE).astype(BF16)
    for c in range(2):
        rows = slice(c * AW, (c + 1) * AW)
        kv = _dot_nt(wt_ref[AW + c * AW:2 * AW + c * AW, :], h)
        kv_ref[rows, :] = kv
        kvb_ref[rows, :] = kv.astype(BF16)
    f = _dot_nt(wft_ref[...], h)[0:N_HEADS, :]
    lf_ref[...] = _log_sigmoid(f + bf_ref[...])


def fox_project_prompt(x, g, wt_qkv, wt_f, b_f, kv_out, layer):
    b, s, d = x.shape
    tm = ROW_TILE
    return pl.pallas_call(
        _fox_proj_prompt_kernel,
        grid=(b, s // tm),
        in_specs=[
            pl.BlockSpec((None, tm, d), lambda bi, i: (bi, i, 0)),
            pl.BlockSpec((1, d), lambda bi, i: (0, 0)),
            pl.BlockSpec((3 * AW, d), lambda bi, i: (0, 0)),
            pl.BlockSpec((LANES, d), lambda bi, i: (0, 0)),
            pl.BlockSpec((N_HEADS, 1), lambda bi, i: (0, 0)),
            pl.BlockSpec(memory_space=pl.ANY),
        ],
        out_specs=[
            pl.BlockSpec((None, tm, AW), lambda bi, i: (bi, i, 0)),
            pl.BlockSpec((None, None, 2 * AW, tm), lambda bi, i: (layer, bi, 0, i)),
            pl.BlockSpec((None, None, 2 * AW, tm), lambda bi, i: (bi, i, 0, 0)),
            pl.BlockSpec((None, N_HEADS, tm), lambda bi, i: (bi, 0, i)),
        ],
        out_shape=[
            jax.ShapeDtypeStruct((b, s, AW), BF16),
            jax.ShapeDtypeStruct(kv_out.shape, F32),
            jax.ShapeDtypeStruct((b, s // tm, 2 * AW, tm), BF16),
            jax.ShapeDtypeStruct((b, N_HEADS, s), F32),
        ],
        input_output_aliases={5: 1},
        compiler_params=_params("parallel", "parallel"),
        name="fox_project_prompt",
    )(x, g.reshape(1, d), wt_qkv, wt_f, b_f.reshape(N_HEADS, 1), kv_out)


def _fox_proj_sample_kernel(x_ref, g_ref, wt_ref, wft_ref, bf_ref, q_ref, kv_ref, lf_ref):
    h = _rmsnorm_bf16(x_ref[...], g_ref[...])
    q_ref[...] = (_dot_nt(h, wt_ref[0:AW, :]) * Q_SCALE).astype(BF16)
    kv_ref[...] = _dot_nt(h, wt_ref[AW:3 * AW, :])
    f = _dot_nt(h, wft_ref[...])[:, 0:N_HEADS]
    lf_ref[...] = _log_sigmoid(f + bf_ref[...])


def fox_project_sample(x, g, wt_qkv, wt_f, b_f):
    rows, d = x.shape
    return pl.pallas_call(
        _fox_proj_sample_kernel,
        grid=(1,),
        in_specs=[
            pl.BlockSpec((rows, d), lambda i: (0, 0)),
            pl.BlockSpec((1, d), lambda i: (0, 0)),
            pl.BlockSpec((3 * AW, d), lambda i: (0, 0)),
            pl.BlockSpec((LANES, d), lambda i: (0, 0)),
            pl.BlockSpec((1, N_HEADS), lambda i: (0, 0)),
        ],
        out_specs=[
            pl.BlockSpec((rows, AW), lambda i: (0, 0)),
            pl.BlockSpec((rows, 2 * AW), lambda i: (0, 0)),
            pl.BlockSpec((rows, N_HEADS), lambda i: (0, 0)),
        ],
        out_shape=[
            jax.ShapeDtypeStruct((rows, AW), BF16),
            jax.ShapeDtypeStruct((rows, 2 * AW), F32),
            jax.ShapeDtypeStruct((rows, N_HEADS), F32),
        ],
        compiler_params=_params("arbitrary"),
        name="fox_project_sample",
    )(x, g.reshape(1, d), wt_qkv, wt_f, b_f.reshape(1, N_HEADS))


def _cumsum_kernel(lf_ref, c_ref, carry_ref):
    @pl.when(pl.program_id(1) == 0)
    def _():
        carry_ref[...] = jnp.zeros_like(carry_ref)

    ts = lf_ref.shape[1]
    src = lax.broadcasted_iota(jnp.int32, (ts, ts), 0)
    dst = lax.broadcasted_iota(jnp.int32, (ts, ts), 1)
    upper = (src <= dst).astype(F32)
    c = jnp.dot(lf_ref[...], upper, precision=lax.Precision.HIGHEST, preferred_element_type=F32)
    c = c + carry_ref[:, 0:1]
    c_ref[...] = c
    carry_ref[...] = jnp.broadcast_to(c[:, ts - 1:ts], carry_ref.shape)


def cumsum_seq(lf_t, *, ts=512):
    b, h, s = lf_t.shape
    return pl.pallas_call(
        _cumsum_kernel,
        grid=(b, s // ts),
        in_specs=[pl.BlockSpec((None, h, ts), lambda bi, i: (bi, 0, i))],
        out_specs=pl.BlockSpec((None, h, ts), lambda bi, i: (bi, 0, i)),
        out_shape=jax.ShapeDtypeStruct((b, h, s), F32),
        scratch_shapes=[pltpu.VMEM((h, LANES), F32)],
        compiler_params=_params("parallel", "arbitrary"),
        name="cumsum_seq",
    )(lf_t)


def _fox_prompt_kernel(q_ref, kt_ref, vt_ref, cq_ref, ck_ref, o_ref, *, tq, rq):
    i = pl.program_id(2)
    nsub = tq // rq
    q = q_ref[...]
    lane = lax.broadcasted_iota(jnp.int32, (1, LANES), 1)
    qm = [_select_head(q, hh) for hh in range(HEADS_PER_BLOCK)]
    cq = [cq_ref[:, hh:hh + 1] for hh in range(HEADS_PER_BLOCK)]
    ones = jnp.ones((HEAD_DIM, tq), BF16)

    def tile(j, carry, masked):
        kt = kt_ref[j]
        vt = vt_ref[j]
        vts = [jnp.concatenate([vt[0:HEAD_DIM], ones], axis=0), jnp.concatenate([ones, vt[HEAD_DIM:]], axis=0)]
        new = []
        for hh in range(HEADS_PER_BLOCK):
            ck = ck_ref[j, hh:hh + 1, :]
            subs = []
            for sb in range(nsub):
                m, acc = carry[hh][sb]
                rows = slice(sb * rq, (sb + 1) * rq)
                s = jnp.dot(qm[hh][rows], kt, preferred_element_type=F32) + (cq[hh][rows] - ck)
                if masked:
                    row = lax.broadcasted_iota(jnp.int32, (rq, tq), 0) + sb * rq
                    col = lax.broadcasted_iota(jnp.int32, (rq, tq), 1)
                    s = jnp.where(col <= row, s, NEG_BIG)
                m_new = jnp.maximum(m, jnp.max(s, axis=-1, keepdims=True))
                p = jnp.exp(s - m_new)
                acc = jnp.exp(m - m_new) * acc + _dot_nt(p.astype(BF16), vts[hh])
                subs.append((m_new, acc))
            new.append(tuple(subs))
        return tuple(new)

    init = tuple(tuple((jnp.full((rq, 1), NEG_BIG, F32), jnp.zeros((rq, LANES), F32)) for _ in range(nsub))
                 for _ in range(HEADS_PER_BLOCK))
    carry = lax.fori_loop(0, i, lambda j, c: tile(j, c, False), init)
    carry = tile(i, carry, True)
    for sb in range(nsub):
        acc0, acc1 = carry[0][sb][1], carry[1][sb][1]
        out = jnp.where(lane < HEAD_DIM, acc0 / acc0[:, HEAD_DIM:HEAD_DIM + 1], acc1 / acc1[:, 0:1])
        o_ref[sb * rq:(sb + 1) * rq, :] = out.astype(o_ref.dtype)


def fox_prompt_attention(q, kvb, c_cols, c_rows):
    b, s, _ = q.shape
    tq = ROW_TILE
    nt = s // tq
    return pl.pallas_call(
        functools.partial(_fox_prompt_kernel, tq=tq, rq=FOX_SUB),
        grid=(b, N_PAIRS, nt),
        in_specs=[
            pl.BlockSpec((None, tq, LANES), lambda bi, hp, i: (bi, i, hp)),
            pl.BlockSpec((None, nt, LANES, tq), lambda bi, hp, i: (bi, 0, hp, 0)),
            pl.BlockSpec((None, nt, LANES, tq), lambda bi, hp, i: (bi, 0, N_PAIRS + hp, 0)),
            pl.BlockSpec((None, None, tq, HEADS_PER_BLOCK), lambda bi, hp, i: (bi, hp, i, 0)),
            pl.BlockSpec((None, None, nt, HEADS_PER_BLOCK, tq), lambda bi, hp, i: (bi, hp, 0, 0, 0)),
        ],
        out_specs=pl.BlockSpec((None, tq, LANES), lambda bi, hp, i: (bi, i, hp)),
        out_shape=jax.ShapeDtypeStruct((b, s, AW), BF16),
        compiler_params=_params("parallel", "parallel", "arbitrary"),
        name="fox_prompt_attention",
    )(q, kvb, kvb, c_cols, c_rows)


def _head_mask(rows):
    head = lax.broadcasted_iota(jnp.int32, (rows, AW), 1) // HEAD_DIM
    return head == lax.broadcasted_iota(jnp.int32, (rows, AW), 0)


def _row_to_col(row):
    eye = (lax.broadcasted_iota(jnp.int32, (N_HEADS, N_HEADS), 0)
           == lax.broadcasted_iota(jnp.int32, (N_HEADS, N_HEADS), 1))
    return jnp.sum(jnp.where(eye, jnp.broadcast_to(row, (N_HEADS, N_HEADS)), 0.0), axis=1, keepdims=True)


def _fox_sample_kernel(pt_ref, q_ref, kvn_ref, lfn_ref, *refs, n_new, pps):
    del pt_ref
    kv_refs, lf_refs, o_ref = refs[0:pps], refs[pps:2 * pps], refs[2 * pps]
    qbd_ref, qbdb_ref, cn_ref, m_ref, l_ref, acc_ref, carry_ref = refs[2 * pps + 1:]
    p = pl.program_id(1)
    nrow = n_new * N_HEADS
    mask16 = _head_mask(N_HEADS)

    @pl.when(p == 0)
    def _():
        run = jnp.zeros((1, N_HEADS), F32)
        for t in range(n_new):
            rows = slice(t * N_HEADS, (t + 1) * N_HEADS)
            qt = jnp.where(mask16, jnp.broadcast_to(q_ref[t:t + 1, :], (N_HEADS, AW)), 0.0)
            qbd_ref[rows, :] = qt
            qbdb_ref[rows, :] = qt.astype(BF16)
            run = run + lfn_ref[t:t + 1, :]
            cn_ref[rows, :] = jnp.broadcast_to(_row_to_col(run), (N_HEADS, LANES))
        m_ref[...] = jnp.full_like(m_ref, NEG_BIG)
        l_ref[...] = jnp.zeros_like(l_ref)
        acc_ref[...] = jnp.zeros_like(acc_ref)
        carry_ref[...] = jnp.zeros_like(carry_ref)

    page = lf_refs[0].shape[1]
    src = lax.broadcasted_iota(jnp.int32, (page, page), 0)
    dst = lax.broadcasted_iota(jnp.int32, (page, page), 1)
    later = (src > dst).astype(F32)
    cn = cn_ref[:, 0:1]
    qbd_b = qbdb_ref[...]
    carry = carry_ref[:, 0:1]
    scores, values = [], []
    for k in range(pps):
        lf_t = lf_refs[k][...]
        suffix = jnp.dot(lf_t, later, precision=lax.Precision.HIGHEST, preferred_element_type=F32) + carry
        carry = suffix[:, 0:1] + lf_t[:, 0:1]
        kt = kv_refs[k][0:AW, :].astype(BF16)
        s = jnp.dot(qbd_b, kt, preferred_element_type=F32)
        scores.append(s + (jnp.concatenate([suffix] * n_new, axis=0) + cn))
        values.append(kv_refs[k][AW:2 * AW, :].astype(BF16))
    carry_ref[...] = jnp.broadcast_to(carry, carry_ref.shape)
    m_old = m_ref[:, 0:1]
    m_new = m_old
    for s in scores:
        m_new = jnp.maximum(m_new, jnp.max(s, axis=-1, keepdims=True))
    alpha = jnp.exp(m_old - m_new)
    l_new = alpha * l_ref[:, 0:1]
    acc_new = alpha * acc_ref[...]
    for s, vt in zip(scores, values):
        pr = jnp.exp(s - m_new)
        l_new = l_new + jnp.sum(pr, axis=-1, keepdims=True)
        acc_new = acc_new + _dot_nt(pr.astype(BF16), vt)
    m_ref[...] = jnp.broadcast_to(m_new, m_ref.shape)
    l_ref[...] = jnp.broadcast_to(l_new, l_ref.shape)
    acc_ref[...] = acc_new

    @pl.when(p == pl.num_programs(1) - 1)
    def _():
        qbd = qbd_ref[...]
        tok = lax.broadcasted_iota(jnp.int32, (nrow, LANES), 0)[:, 0:1] // N_HEADS
        s_new = []
        for t2 in range(n_new):
            kn = kvn_ref[t2:t2 + 1, 0:AW].astype(BF16).astype(F32)
            st = jnp.sum(qbd * kn, axis=-1, keepdims=True)
            cn_t2 = jnp.concatenate([cn_ref[t2 * N_HEADS:(t2 + 1) * N_HEADS, :]] * n_new, axis=0)[:, 0:1]
            st = st + cn - cn_t2
            s_new.append(jnp.where(tok >= t2, st, NEG_BIG))
        m_fin = m_new
        for st in s_new:
            m_fin = jnp.maximum(m_fin, st)
        beta = jnp.exp(m_new - m_fin)
        l_fin = beta * l_new
        acc_fin = beta * acc_new
        for t2 in range(n_new):
            pn = jnp.exp(s_new[t2] - m_fin)
            l_fin = l_fin + pn
            vn = kvn_ref[t2:t2 + 1, AW:2 * AW].astype(BF16).astype(F32)
            acc_fin = acc_fin + pn.astype(BF16).astype(F32) * vn
        out = acc_fin / l_fin
        for t in range(n_new):
            blk = jnp.where(mask16, out[t * N_HEADS:(t + 1) * N_HEADS, :], 0.0)
            o_ref[t:t + 1, :] = jnp.sum(blk, axis=0, keepdims=True)


def fox_sample_attention(page_table, q, kv_new, lf_new, cache_kvt, cache_lft, layer):
    db, t, _ = q.shape
    npg = page_table.shape[0] // db
    page = cache_kvt.shape[3]
    nrow = t * N_HEADS
    pps = FOX_PAGES_PER_STEP
    assert npg % pps == 0

    def page_map(k):
        return lambda b, p, pt: (layer, pt[b * npg + (npg - 1 - (p * pps + k))], 0, 0)

    grid_spec = pltpu.PrefetchScalarGridSpec(
        num_scalar_prefetch=1,
        grid=(db, npg // pps),
        in_specs=[
            pl.BlockSpec((None, t, AW), lambda b, p, pt: (b, 0, 0)),
            pl.BlockSpec((None, t, 2 * AW), lambda b, p, pt: (b, 0, 0)),
            pl.BlockSpec((None, t, N_HEADS), lambda b, p, pt: (b, 0, 0)),
        ] + [pl.BlockSpec((None, None, 2 * AW, page), page_map(k)) for k in range(pps)]
        + [pl.BlockSpec((None, None, N_HEADS, page), page_map(k)) for k in range(pps)],
        out_specs=pl.BlockSpec((None, t, AW), lambda b, p, pt: (b, 0, 0)),
        scratch_shapes=[
            pltpu.VMEM((nrow, AW), F32),
            pltpu.VMEM((nrow, AW), BF16),
            pltpu.VMEM((nrow, LANES), F32),
            pltpu.VMEM((nrow, LANES), F32),
            pltpu.VMEM((nrow, LANES), F32),
            pltpu.VMEM((nrow, AW), F32),
            pltpu.VMEM((N_HEADS, LANES), F32),
        ],
    )
    return pl.pallas_call(
        functools.partial(_fox_sample_kernel, n_new=t, pps=pps),
        grid_spec=grid_spec,
        out_shape=jax.ShapeDtypeStruct((db, t, AW), F32),
        compiler_params=_params("parallel", "arbitrary"),
        name="fox_sample_attention",
    )(page_table, q, kv_new, lf_new, *([cache_kvt] * pps), *([cache_lft] * pps))


def _dilated_prompt_kernel(slope_ref, q_ref, kp_ref, vp_ref, kc_ref, vc_ref, o_ref, lse_ref):
    u = pl.program_id(2)
    tu = q_ref.shape[0]
    lane = lax.broadcasted_iota(jnp.int32, (1, LANES), 1)
    iq = lax.broadcasted_iota(jnp.int32, (tu, 2 * tu), 0)
    ik = lax.broadcasted_iota(jnp.int32, (tu, 2 * tu), 1)
    taps = iq + tu - ik
    dist = taps.astype(F32)
    ok = jnp.logical_and(jnp.logical_and(taps >= 0, taps <= tu), jnp.logical_or(ik >= tu, u > 0))
    lse_all = jnp.zeros((tu, LANES), F32)
    for hp in range(N_PAIRS):
        sl = slice(hp * LANES, (hp + 1) * LANES)
        q = q_ref[:, sl]
        k = jnp.concatenate([kp_ref[:, sl], kc_ref[:, sl]], axis=0)
        v = jnp.concatenate([vp_ref[:, sl], vc_ref[:, sl]], axis=0)
        outs = []
        for hh in range(HEADS_PER_BLOCK):
            head = hp * HEADS_PER_BLOCK + hh
            slope = slope_ref[head]
            s = _dot_nt(_select_head(q, hh), k)
            s = jnp.where(ok, s - slope * dist, NEG_BIG)
            m = jnp.max(s, axis=-1, keepdims=True)
            p = jnp.exp(s - m)
            l = jnp.sum(p, axis=-1, keepdims=True)
            acc = jnp.dot(p.astype(BF16), v, preferred_element_type=F32)
            outs.append(acc / l)
            lse_all = jnp.where(lane == head, m + jnp.log(l), lse_all)
        o_ref[:, sl] = jnp.where(lane < HEAD_DIM, outs[0], outs[1]).astype(o_ref.dtype)
    lse_ref[...] = lse_all


def dilated_prompt_branch(q_g, kv_g, slopes_d, g):
    b, d, su, _ = q_g.shape
    tu = DIL_TILE

    def prev(u):
        return jnp.maximum(u - 1, 0)

    blk = (None, None, tu, AW)
    return pl.pallas_call(
        _dilated_prompt_kernel,
        grid=(b, d, su // tu),
        in_specs=[
            pl.BlockSpec(memory_space=pltpu.SMEM),
            pl.BlockSpec(blk, lambda bi, r, u: (bi, r, u, 0)),
            pl.BlockSpec(blk, lambda bi, r, u: (bi, r, prev(u), 0)),
            pl.BlockSpec(blk, lambda bi, r, u: (bi, r, prev(u), 1)),
            pl.BlockSpec(blk, lambda bi, r, u: (bi, r, u, 0)),
            pl.BlockSpec(blk, lambda bi, r, u: (bi, r, u, 1)),
        ],
        out_specs=[
            pl.BlockSpec(blk, lambda bi, r, u: (bi, r, u, 0)),
            pl.BlockSpec((None, None, tu, LANES), lambda bi, r, u: (bi, r, u, 0)),
        ],
        out_shape=[
            jax.ShapeDtypeStruct((b, d, su, AW), BF16),
            jax.ShapeDtypeStruct((b, d, su, LANES), F32),
        ],
        compiler_params=_params("parallel", "parallel", "arbitrary"),
        name=f"dilated_prompt_branch{g}",
    )(slopes_d, q_g, kv_g, kv_g, kv_g, kv_g)


def _merge_proj_residual_kernel(x_ref, o0_ref, o1_ref, o2_ref, l0_ref, l1_ref, l2_ref,
                                u1_ref, u2_ref, u1f_ref, u2f_ref, e_ref, w_ref, y_ref):
    tm = x_ref.shape[0]
    o_refs, l_refs = (o0_ref, o1_ref, o2_ref), (l0_ref, l1_ref, l2_ref)
    undo_b, undo_f = (None, u1_ref, u2_ref), (None, u1f_ref, u2f_ref)
    outs, lses = [], []
    for g in range(N_BRANCH):
        o = o_refs[g][...].reshape(tm, AW)
        lse = l_refs[g][...].reshape(tm, LANES)
        if undo_b[g] is not None:
            o = jnp.dot(undo_b[g][...], o, preferred_element_type=F32)
            lse = jnp.dot(undo_f[g][...], lse, precision=lax.Precision.HIGHEST, preferred_element_type=F32)
        outs.append(o.astype(F32))
        lses.append(lse)
    m = jnp.maximum(jnp.maximum(lses[0], lses[1]), lses[2])
    es = [jnp.exp(l - m) for l in lses]
    inv = 1.0 / (es[0] + es[1] + es[2])
    merged = jnp.zeros((tm, AW), F32)
    for g in range(N_BRANCH):
        w = es[g] * inv
        w_hi = w.astype(BF16)
        w_lo = (w - w_hi.astype(F32)).astype(BF16)
        w_wide = (jnp.dot(w_hi, e_ref[...], preferred_element_type=F32)
                  + jnp.dot(w_lo, e_ref[...], preferred_element_type=F32))
        merged = merged + w_wide * outs[g]
    y_ref[...] = x_ref[...] + jnp.dot(merged.astype(BF16), w_ref[...], preferred_element_type=F32)


def merge_proj_residual(x, outs, lses, w):
    b, s, d = x.shape
    tm = ROW_TILE
    undo = [_deinterleave_matrix(tm, dil).T for dil in DILATIONS[1:]]
    expand = np.zeros((LANES, AW), np.float32)
    expand[np.arange(AW) // HEAD_DIM, np.arange(AW)] = 1.0
    const = lambda shape: pl.BlockSpec(shape, lambda bi, i: (0,) * len(shape))
    grouped = lambda dil, width: pl.BlockSpec((None, dil, tm // dil, width), lambda bi, i: (bi, 0, i, 0))
    return pl.pallas_call(
        _merge_proj_residual_kernel,
        grid=(b, s // tm),
        in_specs=[pl.BlockSpec((None, tm, d), lambda bi, i: (bi, i, 0))]
        + [grouped(dil, AW) for dil in DILATIONS] + [grouped(dil, LANES) for dil in DILATIONS]
        + [const((tm, tm))] * 4 + [const((LANES, AW)), const((AW, d))],
        out_specs=pl.BlockSpec((None, tm, d), lambda bi, i: (bi, i, 0)),
        out_shape=jax.ShapeDtypeStruct((b, s, d), F32),
        compiler_params=_params("parallel", "parallel"),
        name="merge_proj_residual",
    )(x, *outs, *lses, jnp.asarray(undo[0], BF16), jnp.asarray(undo[1], BF16),
      jnp.asarray(undo[0], F32), jnp.asarray(undo[1], F32), jnp.asarray(expand, BF16), w)


def _dilated_sample_kernel(*refs, n_new, write_cache):
    slope_ref, q_ref, kvn_ref = refs[0:3]
    cache_refs = refs[3:6]
    o_ref = refs[6]
    out_refs = refs[7:10] if write_cache else None
    nr = q_ref.shape[1]
    row = lax.broadcasted_iota(jnp.int32, (nr, LANES), 0)
    lane = lax.broadcasted_iota(jnp.int32, (nr, LANES), 1)
    qmask = jnp.logical_and(lane // HEAD_DIM == row // n_new, row < HEADS_PER_BLOCK * n_new)
    t_col = (row % n_new)[:, 0:1]
    slope = slope_ref[:, 0:1]

    scores, values, news = [], [], []
    for g in range(N_BRANCH):
        d = DILATIONS[g]
        kt = cache_refs[g][0]
        vt = cache_refs[g][1]
        length = kt.shape[1]
        qf = jnp.where(qmask, q_ref[g], 0.0)
        s = jnp.dot(qf.astype(BF16), kt.astype(BF16), preferred_element_type=F32)
        pos = lax.broadcasted_iota(jnp.int32, (1, length), 1)
        dist = (length + t_col - pos).astype(F32)
        valid = (pos >= t_col) if d == 1 else ((pos & (d - 1)) == t_col)
        scores.append(jnp.where(valid, s - slope * dist, NEG_BIG))
        values.append(vt.astype(BF16))
        for t2 in range(n_new):
            kn = kvn_ref[g, 0, t2:t2 + 1, :].astype(BF16).astype(F32)
            vn = kvn_ref[g, 1, t2:t2 + 1, :].astype(BF16).astype(F32)
            sn = jnp.sum(qf * kn, axis=-1, keepdims=True)
            ok = (t_col >= t2) if d == 1 else (t_col == t2)
            sn = jnp.where(ok, sn - slope * ((t_col - t2) * d).astype(F32), NEG_BIG)
            news.append((sn, vn))
        if write_cache:
            for c in range(2):
                src = cache_refs[g][c]
                new8 = kvn_ref[g, c]
                padded = jnp.concatenate([new8, jnp.zeros((LANES - new8.shape[0], LANES), F32)], axis=0)
                new_cols = pltpu.roll(padded.T, LANES - n_new, axis=1)
                shifted = pltpu.roll(src, length - n_new, axis=1)
                tail_lane = lax.broadcasted_iota(jnp.int32, (LANES, LANES), 1)
                tail = jnp.where(tail_lane >= LANES - n_new, new_cols, shifted[:, length - LANES:])
                if length > LANES:
                    out_refs[g][c, :, 0:length - LANES] = shifted[:, 0:length - LANES]
                out_refs[g][c, :, length - LANES:] = tail

    m = jnp.max(scores[0], axis=-1, keepdims=True)
    for s in scores[1:]:
        m = jnp.maximum(m, jnp.max(s, axis=-1, keepdims=True))
    for sn, _ in news:
        m = jnp.maximum(m, sn)
    l = jnp.zeros((nr, 1), F32)
    acc = jnp.zeros((nr, LANES), F32)
    for s, vt in zip(scores, values):
        p = jnp.exp(s - m)
        l = l + jnp.sum(p, axis=-1, keepdims=True)
        acc = acc + _dot_nt(p.astype(BF16), vt)
    for sn, vn in news:
        p = jnp.exp(sn - m)
        l = l + p
        acc = acc + p.astype(BF16).astype(F32) * vn
    out = acc / l
    second = pltpu.roll(out, nr - n_new, axis=0)
    o_ref[...] = jnp.where(lane < HEAD_DIM, out, second)[0:o_ref.shape[0], :]


def dilated_sample_attention(q16, kvn, slope_rows, caches_t, *, n_new, write_cache):
    db = q16.shape[0]
    nr = q16.shape[3]
    in_specs = [
        pl.BlockSpec((None, nr, LANES), lambda b, hp: (hp, 0, 0)),
        pl.BlockSpec((None, None, N_BRANCH, nr, LANES), lambda b, hp: (b, hp, 0, 0, 0)),
        pl.BlockSpec((None, None, N_BRANCH, 2, 8, LANES), lambda b, hp: (b, hp, 0, 0, 0, 0)),
    ]
    cache_specs = [pl.BlockSpec((None, 2, LANES, c.shape[3]), lambda b, hp: (b, 0, hp, 0)) for c in caches_t]
    cache_args = list(caches_t)
    out_specs = [pl.BlockSpec((None, None, 8, LANES), lambda b, hp: (b, hp, 0, 0))]
    out_shape = [jax.ShapeDtypeStruct((db, N_PAIRS, 8, LANES), F32)]
    if write_cache:
        out_specs += cache_specs
        out_shape += [jax.ShapeDtypeStruct(c.shape, F32) for c in cache_args]
    outs = pl.pallas_call(
        functools.partial(_dilated_sample_kernel, n_new=n_new, write_cache=write_cache),
        grid=(db, N_PAIRS),
        in_specs=in_specs + cache_specs,
        out_specs=out_specs,
        out_shape=out_shape,
        compiler_params=_params("parallel", "arbitrary"),
        name="dilated_sample_attention" + ("_shift" if write_cache else ""),
    )(slope_rows, q16, kvn, *cache_args)
    return outs


def _proj_residual_kernel(x_ref, o_ref, w_ref, y_ref):
    y_ref[...] = x_ref[...] + jnp.dot(o_ref[...].astype(BF16), w_ref[...], preferred_element_type=F32)


def proj_residual(x, o, w):
    rows, d = x.shape
    tm = min(ROW_TILE, rows)
    return pl.pallas_call(
        _proj_residual_kernel,
        grid=(rows // tm,),
        in_specs=[
            pl.BlockSpec((tm, d), lambda i: (i, 0)),
            pl.BlockSpec((tm, AW), lambda i: (i, 0)),
            pl.BlockSpec((AW, d), lambda i: (0, 0)),
        ],
        out_specs=pl.BlockSpec((tm, d), lambda i: (i, 0)),
        out_shape=jax.ShapeDtypeStruct((rows, d), F32),
        compiler_params=_params("parallel"),
        name="proj_residual",
    )(x, o, w)


def _ffn_kernel(x_ref, g_ref, wg_ref, wu_ref, wd_ref, y_ref, h_ref, acc_ref):
    j = pl.program_id(1)

    @pl.when(j == 0)
    def _():
        h_ref[...] = _rmsnorm_bf16(x_ref[...], g_ref[...])
        acc_ref[...] = jnp.zeros_like(acc_ref)

    h = h_ref[...]
    gate = jnp.dot(h, wg_ref[...], preferred_element_type=F32)
    up = jnp.dot(h, wu_ref[...], preferred_element_type=F32)
    act = gate * (1.0 / (1.0 + jnp.exp(-gate))) * up
    acc_ref[...] += jnp.dot(act.astype(BF16), wd_ref[...], preferred_element_type=F32)

    @pl.when(j == pl.num_programs(1) - 1)
    def _():
        y_ref[...] = x_ref[...] + acc_ref[...]


def ffn_residual(x, g, w_gate_up, w_down, *, tm=1024, tf=256):
    rows, d = x.shape
    f = w_down.shape[0]
    tm = min(tm, rows)
    nf = f // tf
    return pl.pallas_call(
        _ffn_kernel,
        grid=(rows // tm, nf),
        in_specs=[
            pl.BlockSpec((tm, d), lambda i, j: (i, 0)),
            pl.BlockSpec((1, d), lambda i, j: (0, 0)),
            pl.BlockSpec((d, tf), lambda i, j: (0, j)),
            pl.BlockSpec((d, tf), lambda i, j: (0, nf + j)),
            pl.BlockSpec((tf, d), lambda i, j: (j, 0)),
        ],
        out_specs=pl.BlockSpec((tm, d), lambda i, j: (i, 0)),
        out_shape=jax.ShapeDtypeStruct((rows, d), F32),
        scratch_shapes=[pltpu.VMEM((tm, d), BF16), pltpu.VMEM((tm, d), F32)],
        compiler_params=_params("parallel", "arbitrary"),
        name="ffn_residual",
    )(x, g.reshape(1, d), w_gate_up, w_gate_up, w_down)


def _rmsnorm_kernel(x_ref, g_ref, y_ref):
    x = x_ref[...]
    y = x * lax.rsqrt(jnp.mean(x * x, axis=-1, keepdims=True) + RMS_EPS)
    y_ref[...] = y * g_ref[...]


def rmsnorm(x, g):
    rows, d = x.shape
    tm = min(ROW_TILE, rows)
    return pl.pallas_call(
        _rmsnorm_kernel,
        grid=(rows // tm,),
        in_specs=[pl.BlockSpec((tm, d), lambda i: (i, 0)), pl.BlockSpec((1, d), lambda i: (0, 0))],
        out_specs=pl.BlockSpec((tm, d), lambda i: (i, 0)),
        out_shape=jax.ShapeDtypeStruct((rows, d), F32),
        compiler_params=_params("parallel"),
        name="rmsnorm",
    )(x, g.reshape(1, d))


def _to_token_major(kv_t, lead):
    n = len(lead)
    x = kv_t.reshape(*lead, 2, N_HEADS, HEAD_DIM, kv_t.shape[-1])
    return x.transpose(*range(n), n + 3, n, n + 1, n + 2)


def kernel(x_prompt, x_sample, cache_fox_kv, cache_fox_logf, cache_win0, cache_win1, cache_win2, page_table,
           norm_g, w_qkvf, b_f, wo_a, norm_kv, w_kv_shared, w_q_b, wo_b, w_gate_up, w_down, norm_final):
    b, s, d = x_prompt.shape
    db, t, _ = x_sample.shape
    n_a = w_qkvf.shape[0]
    depth = norm_g.shape[0]
    n_pool, page = cache_fox_kv.shape[1], cache_fox_kv.shape[2]
    caches = (cache_win0, cache_win1, cache_win2)
    assert d == AW and s % (DILATIONS[-1] * DIL_TILE) == 0 and s % ROW_TILE == 0 and t <= DILATIONS[1]
    assert all(c.shape[1] == w for c, w in zip(caches, WINDOWS)) and WINDOWS[-1] <= s
    assert 2 * t <= 8 and ROW_TILE % (DILATIONS[-1] * 16) == 0

    slopes = jnp.exp2(-8.0 * jnp.arange(1, N_HEADS + 1, dtype=F32) / N_HEADS)
    cache_kvt = cache_fox_kv.transpose(0, 1, 3, 4, 5, 2).reshape(n_a, n_pool, 2 * AW, page)
    cache_lft = cache_fox_logf.transpose(0, 1, 3, 2)
    caches_t = [c.transpose(0, 2, 3, 4, 1).reshape(db, 2, AW, c.shape[1]) for c in caches]
    pt_flat = page_table.reshape(-1)
    slope_rows = jnp.pad(jnp.repeat(slopes.reshape(N_PAIRS, HEADS_PER_BLOCK), t, axis=1),
                         ((0, 0), (0, 16 - HEADS_PER_BLOCK * t)))
    slope_rows = jnp.broadcast_to(slope_rows[:, :, None], (N_PAIRS, 16, LANES))

    xp = x_prompt
    xs = x_sample.reshape(db * t, d)
    fox_kv_p = jnp.zeros((n_a, b, 2 * AW, s), F32)
    fox_lf_p, fox_kv_s, fox_lf_s = [], [], []
    tq = ROW_TILE
    for layer in range(depth):
        g1, g2 = norm_g[layer, 0], norm_g[layer, 1]
        if layer < n_a:
            a = layer
            wt = w_qkvf[a].T
            wt_qkv = wt[:3 * AW].astype(BF16)
            wt_f = jnp.pad(wt[3 * AW:], ((0, LANES - N_HEADS), (0, 0))).astype(BF16)
            wo = wo_a[a].astype(BF16)
            q, fox_kv_p, kvb, lf_t = fox_project_prompt(xp, g1, wt_qkv, wt_f, b_f[a], fox_kv_p, a)
            c = cumsum_seq(lf_t).reshape(b, N_PAIRS, HEADS_PER_BLOCK, s)
            c_cols = c.transpose(0, 1, 3, 2)
            c_rows = c.reshape(b, N_PAIRS, HEADS_PER_BLOCK, s // tq, tq).transpose(0, 1, 3, 2, 4)
            o = fox_prompt_attention(q, kvb, c_cols, c_rows)
            xp = proj_residual(xp.reshape(b * s, d), o.reshape(b * s, AW), wo).reshape(b, s, d)
            fox_lf_p.append(lf_t.transpose(0, 2, 1))
            q, kv, lf = fox_project_sample(xs, g1, wt_qkv, wt_f, b_f[a])
            o = fox_sample_attention(pt_flat, q.astype(F32).reshape(db, t, AW), kv.reshape(db, t, 2 * AW),
                                     lf.reshape(db, t, N_HEADS), cache_kvt, cache_lft, a)
            xs = proj_residual(xs, o.reshape(db * t, AW), wo)
            fox_kv_s.append(kv.reshape(db, t, 2, N_HEADS, HEAD_DIM))
            fox_lf_s.append(lf.reshape(db, t, N_HEADS))
        else:
            bl = layer - n_a
            wq = w_q_b[bl].astype(BF16)
            wo = wo_b[bl].astype(BF16)
            outs, lses = [], []
            for g in range(N_BRANCH):
                q_g = norm_matmul_grouped(xp, g1, wq[:, g * AW:(g + 1) * AW], DILATIONS[g], scale=Q_SCALE)
                o_g, lse_g = dilated_prompt_branch(q_g, kvsh_p[g], slopes * float(DILATIONS[g]), g)
                outs.append(o_g)
                lses.append(lse_g)
            xp = merge_proj_residual(xp, outs, lses, wo)
            q3 = norm_matmul(xs, g1, wq, BF16, scale=Q_SCALE).astype(F32)
            q16 = q3.reshape(db, t, N_BRANCH, N_PAIRS, LANES).transpose(0, 3, 2, 1, 4)
            q16 = jnp.concatenate([q16] * HEADS_PER_BLOCK, axis=3)
            q16 = jnp.pad(q16, ((0, 0),) * 3 + ((0, 16 - HEADS_PER_BLOCK * t), (0, 0)))
            res = dilated_sample_attention(q16, kvn, slope_rows, caches_t, n_new=t, write_cache=(bl == 0))
            if bl == 0:
                win_s = [_to_token_major(res[1 + g], (db,)) for g in range(N_BRANCH)]
            o = res[0][:, :, :t].transpose(0, 2, 1, 3).reshape(db * t, AW)
            xs = proj_residual(xs, o, wo)
        wgu = w_gate_up[layer].astype(BF16)
        wd = w_down[layer].astype(BF16)
        xp = ffn_residual(xp.reshape(b * s, d), g2, wgu, wd).reshape(b, s, d)
        xs = ffn_residual(xs, g2, wgu, wd)
        if layer == n_a - 1:
            wkv = w_kv_shared.astype(BF16)
            kvsh_p = [norm_matmul_grouped(xp, norm_kv, wkv[:, g * 2 * AW:(g + 1) * 2 * AW], DILATIONS[g])
                      for g in range(N_BRANCH)]
            tail = WINDOWS[-1]
            x_tail = xp[:, s - tail:].reshape(b * tail, d)
            kv_tail = norm_matmul(x_tail, norm_kv, wkv, F32).reshape(b, tail, N_BRANCH, 2, N_HEADS, HEAD_DIM)
            win_p = [kv_tail[:, tail - WINDOWS[g]:, g] for g in range(N_BRANCH)]
            kvsh_s = norm_matmul(xs, norm_kv, wkv, F32)
            kvn = kvsh_s.reshape(db, t, N_BRANCH, 2, N_PAIRS, LANES).transpose(0, 4, 2, 3, 1, 5)
            kvn = jnp.pad(kvn, ((0, 0),) * 4 + ((0, 8 - t), (0, 0)))
    y_prompt = rmsnorm(xp.reshape(b * s, d), norm_final).reshape(b, s, d)
    y_sample = rmsnorm(xs, norm_final).reshape(db, t, d)
    return (y_prompt, y_sample, _to_token_major(fox_kv_p, (n_a, b)), jnp.stack(fox_lf_p), jnp.stack(fox_kv_s),
            jnp.stack(fox_lf_s), win_p[0], win_s[0], win_p[1], win_s[1], win_p[2], win_s[2])
```

```python
import functools

import numpy as np
import jax
import jax.numpy as jnp
from jax import lax
from jax.experimental import pallas as pl
from jax.experimental.pallas import tpu as pltpu

F32 = jnp.float32
BF16 = jnp.bfloat16

N_HEADS = 16
HEAD_DIM = 64
AW = N_HEADS * HEAD_DIM
N_BRANCH = 3
WINDOWS = (128, 512, 2048)
DILATIONS = (1, 4, 16)
N_TAPS = 129
RMS_EPS = 1e-6
NEG_BIG = -1e30
LANES = 128
HEADS_PER_BLOCK = LANES // HEAD_DIM
N_PAIRS = N_HEADS // HEADS_PER_BLOCK
Q_SCALE = HEAD_DIM ** -0.5
VMEM_LIMIT = 56 * 1024 * 1024
ROW_TILE = 512
FOX_SUB = 512
DIL_TILE = 128
FOX_PAGES_PER_STEP = 4
SAMPLE_PAIRS_PER_STEP = 2
SKIP_LOGIT = 120.0

_NT = (((1,), (1,)), ((), ()))


def _params(*sem, flags=None):
    return pltpu.CompilerParams(dimension_semantics=sem, vmem_limit_bytes=VMEM_LIMIT, flags=flags)


def _rmsnorm_bf16(x, g):
    y = x * lax.rsqrt(jnp.mean(x * x, axis=-1, keepdims=True) + RMS_EPS)
    return (y * g).astype(BF16)


def _log_sigmoid(x):
    return jnp.minimum(x, 0.0) - jnp.log1p(jnp.exp(-jnp.abs(x)))


def _select_head(q, hh):
    lane = lax.broadcasted_iota(jnp.int32, (1, LANES), 1)
    keep = ((lane // HEAD_DIM) == hh).astype(F32)
    return (q.astype(F32) * keep).astype(BF16)


def _dot_nt(a, b):
    return lax.dot_general(a, b, _NT, preferred_element_type=F32)


def _deinterleave_matrix(n, d):
    p = np.zeros((n, n), np.float32)
    i = np.arange(n)
    p[(i % d) * (n // d) + i // d, i] = 1.0
    return p


def _norm_matmul_kernel(x_ref, g_ref, w_ref, o_ref, h_ref, *, scale):
    @pl.when(pl.program_id(1) == 0)
    def _():
        h_ref[...] = _rmsnorm_bf16(x_ref[...], g_ref[...])

    acc = jnp.dot(h_ref[...], w_ref[...], preferred_element_type=F32)
    o_ref[...] = (acc * scale).astype(o_ref.dtype)


def norm_matmul(x, g, w, out_dtype, *, scale=1.0, tn=512):
    rows, d = x.shape
    n = w.shape[1]
    tm = min(ROW_TILE, rows)
    return pl.pallas_call(
        functools.partial(_norm_matmul_kernel, scale=scale),
        grid=(rows // tm, n // tn),
        in_specs=[
            pl.BlockSpec((tm, d), lambda i, j: (i, 0)),
            pl.BlockSpec((1, d), lambda i, j: (0, 0)),
            pl.BlockSpec((d, tn), lambda i, j: (0, j)),
        ],
        out_specs=pl.BlockSpec((tm, tn), lambda i, j: (i, j)),
        out_shape=jax.ShapeDtypeStruct((rows, n), out_dtype),
        scratch_shapes=[pltpu.VMEM((tm, d), BF16)],
        compiler_params=_params("parallel", "arbitrary"),
        name="norm_matmul",
    )(x, g.reshape(1, d), w)


def _norm_matmul_grouped_kernel(x_ref, g_ref, p_ref, w_ref, o_ref, h_ref, *, scale, dil):
    @pl.when(pl.program_id(2) == 0)
    def _():
        h = _rmsnorm_bf16(x_ref[...], g_ref[...])
        if dil > 1:
            h = jnp.dot(p_ref[...], h, preferred_element_type=F32).astype(BF16)
        h_ref[...] = h

    acc = jnp.dot(h_ref[...], w_ref[...], preferred_element_type=F32) * scale
    o_ref[...] = acc.reshape(o_ref.shape).astype(o_ref.dtype)


def norm_matmul_grouped(x, g, w, dil, *, scale=1.0, tn=512):
    b, s, d = x.shape
    n = w.shape[1]
    tm = ROW_TILE
    perm = jnp.asarray(_deinterleave_matrix(tm, dil), BF16)
    return pl.pallas_call(
        functools.partial(_norm_matmul_grouped_kernel, scale=scale, dil=dil),
        grid=(b, s // tm, n // tn),
        in_specs=[
            pl.BlockSpec((None, tm, d), lambda bi, i, j: (bi, i, 0)),
            pl.BlockSpec((1, d), lambda bi, i, j: (0, 0)),
            pl.BlockSpec((tm, tm), lambda bi, i, j: (0, 0)),
            pl.BlockSpec((d, tn), lambda bi, i, j: (0, j)),
        ],
        out_specs=pl.BlockSpec((None, dil, tm // dil, tn), lambda bi, i, j: (bi, 0, i, j)),
        out_shape=jax.ShapeDtypeStruct((b, dil, s // dil, n), BF16),
        scratch_shapes=[pltpu.VMEM((tm, d), BF16)],
        compiler_params=_params("parallel", "parallel", "arbitrary"),
        name=f"norm_matmul_grouped{dil}",
    )(x, g.reshape(1, d), perm, w)


def _fox_proj_prompt_kernel(x_ref, g_ref, wt_ref, wft_ref, bf_ref, seg_ref, kv_in_ref,
                            q_ref, kv_ref, kvb_ref, lf_ref, qn_ref, kn_ref):
    del kv_in_ref
    tm = x_ref.shape[0]
    h = _rmsnorm_bf16(x_ref[...], g_ref[...])
    q = (_dot_nt(h, wt_ref[0:AW, :]) * Q_SCALE).astype(BF16)
    q_ref[...] = q
    qf = q.astype(F32)
    q_sq = jnp.dot(qf * qf, seg_ref[...], precision=lax.Precision.HIGHEST, preferred_element_type=F32)
    qn_ref[...] = jnp.broadcast_to(jnp.sqrt(jnp.max(q_sq, axis=0, keepdims=True)), qn_ref.shape)
    for c in range(2):
        rows = slice(c * AW, (c + 1) * AW)
        kv = _dot_nt(wt_ref[AW + c * AW:2 * AW + c * AW, :], h)
        kv_ref[rows, :] = kv
        kvb = kv.astype(BF16)
        kvb_ref[rows, :] = kvb
        if c == 0:
            kf = kvb.astype(F32)
            k_sq = jnp.sum((kf * kf).reshape(N_HEADS, HEAD_DIM, tm), axis=1)
            kn_ref[...] = jnp.broadcast_to(jnp.sqrt(jnp.max(k_sq, axis=1, keepdims=True)), kn_ref.shape)
    f = _dot_nt(wft_ref[...], h)[0:N_HEADS, :]
    lf_ref[...] = _log_sigmoid(f + bf_ref[...])


def fox_project_prompt(x, g, wt_qkv, wt_f, b_f, kv_out, layer):
    b, s, d = x.shape
    tm = ROW_TILE
    nt = s // tm
    seg = np.zeros((AW, LANES), np.float32)
    seg[np.arange(AW), np.arange(AW) // HEAD_DIM] = 1.0
    return pl.pallas_call(
        _fox_proj_prompt_kernel,
        grid=(b, nt),
        in_specs=[
            pl.BlockSpec((None, tm, d), lambda bi, i: (bi, i, 0)),
            pl.BlockSpec((1, d), lambda bi, i: (0, 0)),
            pl.BlockSpec((3 * AW, d), lambda bi, i: (0, 0)),
            pl.BlockSpec((LANES, d), lambda bi, i: (0, 0)),
            pl.BlockSpec((N_HEADS, 1), lambda bi, i: (0, 0)),
            pl.BlockSpec((AW, LANES), lambda bi, i: (0, 0)),
            pl.BlockSpec(memory_space=pl.ANY),
        ],
        out_specs=[
            pl.BlockSpec((None, tm, AW), lambda bi, i: (bi, i, 0)),
            pl.BlockSpec((None, None, 2 * AW, tm), lambda bi, i: (layer, bi, 0, i)),
            pl.BlockSpec((None, None, 2 * AW, tm), lambda bi, i: (bi, i, 0, 0)),
            pl.BlockSpec((None, N_HEADS, tm), lambda bi, i: (bi, 0, i)),
            pl.BlockSpec((None, None, 8, LANES), lambda bi, i: (bi, i, 0, 0)),
            pl.BlockSpec((None, None, N_HEADS, LANES), lambda bi, i: (bi, i, 0, 0)),
        ],
        out_shape=[
            jax.ShapeDtypeStruct((b, s, AW), BF16),
            jax.ShapeDtypeStruct(kv_out.shape, F32),
            jax.ShapeDtypeStruct((b, nt, 2 * AW, tm), BF16),
            jax.ShapeDtypeStruct((b, N_HEADS, s), F32),
            jax.ShapeDtypeStruct((b, nt, 8, LANES), F32),
            jax.ShapeDtypeStruct((b, nt, N_HEADS, LANES), F32),
        ],
        input_output_aliases={6: 1},
        compiler_params=_params("parallel", "parallel"),
        name="fox_project_prompt",
    )(x, g.reshape(1, d), wt_qkv, wt_f, b_f.reshape(N_HEADS, 1), jnp.asarray(seg), kv_out)


def _fox_proj_sample_kernel(x_ref, g_ref, wt_ref, wft_ref, bf_ref, q_ref, kv_ref, lf_ref):
    h = _rmsnorm_bf16(x_ref[...], g_ref[...])
    q_ref[...] = (_dot_nt(h, wt_ref[0:AW, :]) * Q_SCALE).astype(BF16)
    kv_ref[...] = _dot_nt(h, wt_ref[AW:3 * AW, :])
    f = _dot_nt(h, wft_ref[...])[:, 0:N_HEADS]
    lf_ref[...] = _log_sigmoid(f + bf_ref[...])


def fox_project_sample(x, g, wt_qkv, wt_f, b_f):
    rows, d = x.shape
    return pl.pallas_call(
        _fox_proj_sample_kernel,
        grid=(1,),
        in_specs=[
            pl.BlockSpec((rows, d), lambda i: (0, 0)),
            pl.BlockSpec((1, d), lambda i: (0, 0)),
            pl.BlockSpec((3 * AW, d), lambda i: (0, 0)),
            pl.BlockSpec((LANES, d), lambda i: (0, 0)),
            pl.BlockSpec((1, N_HEADS), lambda i: (0, 0)),
        ],
        out_specs=[
            pl.BlockSpec((rows, AW), lambda i: (0, 0)),
            pl.BlockSpec((rows, 2 * AW), lambda i: (0, 0)),
            pl.BlockSpec((rows, N_HEADS), lambda i: (0, 0)),
        ],
        out_shape=[
            jax.ShapeDtypeStruct((rows, AW), BF16),
            jax.ShapeDtypeStruct((rows, 2 * AW), F32),
            jax.ShapeDtypeStruct((rows, N_HEADS), F32),
        ],
        compiler_params=_params("arbitrary"),
        name="fox_project_sample",
    )(x, g.reshape(1, d), wt_qkv, wt_f, b_f.reshape(1, N_HEADS))


def _cumsum_kernel(lf_ref, c_ref, cmax_ref, cmin_ref, carry_ref):
    @pl.when(pl.program_id(1) == 0)
    def _():
        carry_ref[...] = jnp.zeros_like(carry_ref)

    ts = lf_ref.shape[1]
    src = lax.broadcasted_iota(jnp.int32, (ts, ts), 0)
    dst = lax.broadcasted_iota(jnp.int32, (ts, ts), 1)
    upper = (src <= dst).astype(F32)
    c = jnp.dot(lf_ref[...], upper, precision=lax.Precision.HIGHEST, preferred_element_type=F32)
    c = c + carry_ref[:, 0:1]
    c_ref[...] = c
    carry_ref[...] = jnp.broadcast_to(c[:, ts - 1:ts], carry_ref.shape)
    cmax_ref[...] = jnp.broadcast_to(jnp.max(c, axis=1, keepdims=True), cmax_ref.shape)
    cmin_ref[...] = jnp.broadcast_to(jnp.min(c, axis=1, keepdims=True), cmin_ref.shape)


def cumsum_seq(lf_t):
    b, h, s = lf_t.shape
    ts = ROW_TILE
    stat = pl.BlockSpec((None, None, h, LANES), lambda bi, i: (bi, i, 0, 0))
    return pl.pallas_call(
        _cumsum_kernel,
        grid=(b, s // ts),
        in_specs=[pl.BlockSpec((None, h, ts), lambda bi, i: (bi, 0, i))],
        out_specs=[pl.BlockSpec((None, h, ts), lambda bi, i: (bi, 0, i)), stat, stat],
        out_shape=[jax.ShapeDtypeStruct((b, h, s), F32)]
        + [jax.ShapeDtypeStruct((b, s // ts, h, LANES), F32)] * 2,
        scratch_shapes=[pltpu.VMEM((h, LANES), F32)],
        compiler_params=_params("parallel", "arbitrary"),
        name="cumsum_seq",
    )(lf_t)


def _fox_prompt_kernel(first_ref, q_ref, kt_ref, vt_ref, cq_ref, ck_ref, o_ref, *, tq, rq):
    i = pl.program_id(2)
    first = first_ref[(pl.program_id(0) * pl.num_programs(1) + pl.program_id(1)) * pl.num_programs(2) + i]
    nsub = tq // rq
    q = q_ref[...]
    lane = lax.broadcasted_iota(jnp.int32, (1, LANES), 1)
    qm = [_select_head(q, hh) for hh in range(HEADS_PER_BLOCK)]
    cq = [cq_ref[:, hh:hh + 1] for hh in range(HEADS_PER_BLOCK)]
    ones = jnp.ones((HEAD_DIM, tq), BF16)

    def tile(j, carry, masked):
        kt = kt_ref[j]
        vt = vt_ref[j]
        vts = [jnp.concatenate([vt[0:HEAD_DIM], ones], axis=0), jnp.concatenate([ones, vt[HEAD_DIM:]], axis=0)]
        new = []
        for hh in range(HEADS_PER_BLOCK):
            ck = ck_ref[j, hh:hh + 1, :]
            subs = []
            for sb in range(nsub):
                m, acc = carry[hh][sb]
                rows = slice(sb * rq, (sb + 1) * rq)
                s = jnp.dot(qm[hh][rows], kt, preferred_element_type=F32) + (cq[hh][rows] - ck)
                if masked:
                    row = lax.broadcasted_iota(jnp.int32, (rq, tq), 0) + sb * rq
                    col = lax.broadcasted_iota(jnp.int32, (rq, tq), 1)
                    s = jnp.where(col <= row, s, NEG_BIG)
                m_new = jnp.maximum(m, jnp.max(s, axis=-1, keepdims=True))
                p = jnp.exp(s - m_new)
                acc = jnp.exp(m - m_new) * acc + _dot_nt(p.astype(BF16), vts[hh])
                subs.append((m_new, acc))
            new.append(tuple(subs))
        return tuple(new)

    init = tuple(tuple((jnp.full((rq, 1), NEG_BIG, F32), jnp.zeros((rq, LANES), F32)) for _ in range(nsub))
                 for _ in range(HEADS_PER_BLOCK))
    carry = lax.fori_loop(first, i, lambda j, c: tile(j, c, False), init)
    carry = tile(i, carry, True)
    for sb in range(nsub):
        acc0, acc1 = carry[0][sb][1], carry[1][sb][1]
        out = jnp.where(lane < HEAD_DIM, acc0 / acc0[:, HEAD_DIM:HEAD_DIM + 1], acc1 / acc1[:, 0:1])
        o_ref[sb * rq:(sb + 1) * rq, :] = out.astype(o_ref.dtype)


def first_needed_tile(qn, kn, cmax, cmin):
    nt = qn.shape[1]
    bound = (qn[:, :, None, :] * (kn[:, None, :, :] + kn[:, :, None, :])
             + cmax[:, :, None, :] - cmin[:, None, :, :])
    needed = bound >= -SKIP_LOGIT
    needed = jnp.any(needed.reshape(needed.shape[:3] + (N_PAIRS, HEADS_PER_BLOCK)), axis=-1)
    tile_j = jnp.arange(nt, dtype=jnp.int32)[None, None, :, None]
    tile_i = jnp.arange(nt, dtype=jnp.int32)[None, :, None, None]
    first = jnp.min(jnp.where(jnp.logical_and(needed, tile_j < tile_i), tile_j, tile_i), axis=2)
    return first.transpose(0, 2, 1).reshape(-1).astype(jnp.int32)


def fox_prompt_attention(first, q, kvb, c_cols, c_rows):
    b, s, _ = q.shape
    tq = ROW_TILE
    nt = s // tq
    grid_spec = pltpu.PrefetchScalarGridSpec(
        num_scalar_prefetch=1,
        grid=(b, N_PAIRS, nt),
        in_specs=[
            pl.BlockSpec((None, tq, LANES), lambda bi, hp, i, f: (bi, i, hp)),
            pl.BlockSpec((None, nt, LANES, tq), lambda bi, hp, i, f: (bi, 0, hp, 0)),
            pl.BlockSpec((None, nt, LANES, tq), lambda bi, hp, i, f: (bi, 0, N_PAIRS + hp, 0)),
            pl.BlockSpec((None, None, tq, HEADS_PER_BLOCK), lambda bi, hp, i, f: (bi, hp, i, 0)),
            pl.BlockSpec((None, None, nt, HEADS_PER_BLOCK, tq), lambda bi, hp, i, f: (bi, hp, 0, 0, 0)),
        ],
        out_specs=pl.BlockSpec((None, tq, LANES), lambda bi, hp, i, f: (bi, i, hp)),
    )
    return pl.pallas_call(
        functools.partial(_fox_prompt_kernel, tq=tq, rq=FOX_SUB),
        grid_spec=grid_spec,
        out_shape=jax.ShapeDtypeStruct((b, s, AW), BF16),
        compiler_params=_params("parallel", "parallel", "arbitrary"),
        name="fox_prompt_attention",
    )(first, q, kvb, kvb, c_cols, c_rows)


def _head_mask(rows):
    head = lax.broadcasted_iota(jnp.int32, (rows, AW), 1) // HEAD_DIM
    return head == lax.broadcasted_iota(jnp.int32, (rows, AW), 0)


def _row_to_col(row):
    eye = (lax.broadcasted_iota(jnp.int32, (N_HEADS, N_HEADS), 0)
           == lax.broadcasted_iota(jnp.int32, (N_HEADS, N_HEADS), 1))
    return jnp.sum(jnp.where(eye, jnp.broadcast_to(row, (N_HEADS, N_HEADS)), 0.0), axis=1, keepdims=True)


def _fox_sample_kernel(pt_ref, q_ref, kvn_ref, lfn_ref, *refs, n_new, pps):
    del pt_ref
    kv_refs, lf_refs, o_ref = refs[0:pps], refs[pps:2 * pps], refs[2 * pps]
    qbd_ref, qbdb_ref, cn_ref, m_ref, l_ref, acc_ref, carry_ref = refs[2 * pps + 1:]
    p = pl.program_id(1)
    nrow = n_new * N_HEADS
    mask16 = _head_mask(N_HEADS)

    @pl.when(p == 0)
    def _():
        run = jnp.zeros((1, N_HEADS), F32)
        for t in range(n_new):
            rows = slice(t * N_HEADS, (t + 1) * N_HEADS)
            qt = jnp.where(mask16, jnp.broadcast_to(q_ref[t:t + 1, :], (N_HEADS, AW)), 0.0)
            qbd_ref[rows, :] = qt
            qbdb_ref[rows, :] = qt.astype(BF16)
            run = run + lfn_ref[t:t + 1, :]
            cn_ref[rows, :] = jnp.broadcast_to(_row_to_col(run), (N_HEADS, LANES))
        m_ref[...] = jnp.full_like(m_ref, NEG_BIG)
        l_ref[...] = jnp.zeros_like(l_ref)
        acc_ref[...] = jnp.zeros_like(acc_ref)
        carry_ref[...] = jnp.zeros_like(carry_ref)

    page = lf_refs[0].shape[1]
    src = lax.broadcasted_iota(jnp.int32, (page, page), 0)
    dst = lax.broadcasted_iota(jnp.int32, (page, page), 1)
    later = (src > dst).astype(F32)
    cn = cn_ref[:, 0:1]
    qbd_b = qbdb_ref[...]
    carry = carry_ref[:, 0:1]
    scores, values = [], []
    for k in range(pps):
        lf_t = lf_refs[k][...]
        suffix = jnp.dot(lf_t, later, precision=lax.Precision.HIGHEST, preferred_element_type=F32) + carry
        carry = suffix[:, 0:1] + lf_t[:, 0:1]
        kt = kv_refs[k][0:AW, :].astype(BF16)
        s = jnp.dot(qbd_b, kt, preferred_element_type=F32)
        scores.append(s + (jnp.concatenate([suffix] * n_new, axis=0) + cn))
        values.append(kv_refs[k][AW:2 * AW, :].astype(BF16))
    carry_ref[...] = jnp.broadcast_to(carry, carry_ref.shape)
    m_old = m_ref[:, 0:1]
    m_new = m_old
    for s in scores:
        m_new = jnp.maximum(m_new, jnp.max(s, axis=-1, keepdims=True))
    alpha = jnp.exp(m_old - m_new)
    l_new = alpha * l_ref[:, 0:1]
    acc_new = alpha * acc_ref[...]
    for s, vt in zip(scores, values):
        pr = jnp.exp(s - m_new)
        l_new = l_new + jnp.sum(pr, axis=-1, keepdims=True)
        acc_new = acc_new + _dot_nt(pr.astype(BF16), vt)
    m_ref[...] = jnp.broadcast_to(m_new, m_ref.shape)
    l_ref[...] = jnp.broadcast_to(l_new, l_ref.shape)
    acc_ref[...] = acc_new

    @pl.when(p == pl.num_programs(1) - 1)
    def _():
        qbd = qbd_ref[...]
        tok = lax.broadcasted_iota(jnp.int32, (nrow, LANES), 0)[:, 0:1] // N_HEADS
        s_new = []
        for t2 in range(n_new):
            kn = kvn_ref[t2:t2 + 1, 0:AW].astype(BF16).astype(F32)
            st = jnp.sum(qbd * kn, axis=-1, keepdims=True)
            cn_t2 = jnp.concatenate([cn_ref[t2 * N_HEADS:(t2 + 1) * N_HEADS, :]] * n_new, axis=0)[:, 0:1]
            st = st + cn - cn_t2
            s_new.append(jnp.where(tok >= t2, st, NEG_BIG))
        m_fin = m_new
        for st in s_new:
            m_fin = jnp.maximum(m_fin, st)
        beta = jnp.exp(m_new - m_fin)
        l_fin = beta * l_new
        acc_fin = beta * acc_new
        for t2 in range(n_new):
            pn = jnp.exp(s_new[t2] - m_fin)
            l_fin = l_fin + pn
            vn = kvn_ref[t2:t2 + 1, AW:2 * AW].astype(BF16).astype(F32)
            acc_fin = acc_fin + pn.astype(BF16).astype(F32) * vn
        out = acc_fin / l_fin
        for t in range(n_new):
            blk = jnp.where(mask16, out[t * N_HEADS:(t + 1) * N_HEADS, :], 0.0)
            o_ref[t:t + 1, :] = jnp.sum(blk, axis=0, keepdims=True)


def fox_sample_attention(page_table, q, kv_new, lf_new, cache_kvt, cache_lft, layer):
    db, t, _ = q.shape
    npg = page_table.shape[0] // db
    page = cache_kvt.shape[3]
    nrow = t * N_HEADS
    pps = FOX_PAGES_PER_STEP
    assert npg % pps == 0

    def page_map(k):
        return lambda b, p, pt: (layer, pt[b * npg + (npg - 1 - (p * pps + k))], 0, 0)

    grid_spec = pltpu.PrefetchScalarGridSpec(
        num_scalar_prefetch=1,
        grid=(db, npg // pps),
        in_specs=[
            pl.BlockSpec((None, t, AW), lambda b, p, pt: (b, 0, 0)),
            pl.BlockSpec((None, t, 2 * AW), lambda b, p, pt: (b, 0, 0)),
            pl.BlockSpec((None, t, N_HEADS), lambda b, p, pt: (b, 0, 0)),
        ] + [pl.BlockSpec((None, None, 2 * AW, page), page_map(k)) for k in range(pps)]
        + [pl.BlockSpec((None, None, N_HEADS, page), page_map(k)) for k in range(pps)],
        out_specs=pl.BlockSpec((None, t, AW), lambda b, p, pt: (b, 0, 0)),
        scratch_shapes=[
            pltpu.VMEM((nrow, AW), F32),
            pltpu.VMEM((nrow, AW), BF16),
            pltpu.VMEM((nrow, LANES), F32),
            pltpu.VMEM((nrow, LANES), F32),
            pltpu.VMEM((nrow, LANES), F32),
            pltpu.VMEM((nrow, AW), F32),
            pltpu.VMEM((N_HEADS, LANES), F32),
        ],
    )
    return pl.pallas_call(
        functools.partial(_fox_sample_kernel, n_new=t, pps=pps),
        grid_spec=grid_spec,
        out_shape=jax.ShapeDtypeStruct((db, t, AW), F32),
        compiler_params=_params("parallel", "arbitrary"),
        name="fox_sample_attention",
    )(page_table, q, kv_new, lf_new, *([cache_kvt] * pps), *([cache_lft] * pps))


def _dilated_prompt_kernel(slope_ref, q_ref, kp_ref, vp_ref, kc_ref, vc_ref, o_ref, lse_ref):
    u = pl.program_id(2)
    tu = q_ref.shape[0]
    lane = lax.broadcasted_iota(jnp.int32, (1, LANES), 1)
    iq = lax.broadcasted_iota(jnp.int32, (tu, 2 * tu), 0)
    ik = lax.broadcasted_iota(jnp.int32, (tu, 2 * tu), 1)
    taps = iq + tu - ik
    dist = taps.astype(F32)
    ok = jnp.logical_and(jnp.logical_and(taps >= 0, taps <= tu), jnp.logical_or(ik >= tu, u > 0))
    lse_all = jnp.zeros((tu, LANES), F32)
    for hp in range(N_PAIRS):
        sl = slice(hp * LANES, (hp + 1) * LANES)
        q = q_ref[:, sl]
        k = jnp.concatenate([kp_ref[:, sl], kc_ref[:, sl]], axis=0)
        v = jnp.concatenate([vp_ref[:, sl], vc_ref[:, sl]], axis=0)
        outs = []
        for hh in range(HEADS_PER_BLOCK):
            head = hp * HEADS_PER_BLOCK + hh
            slope = slope_ref[head]
            s = _dot_nt(_select_head(q, hh), k)
            s = jnp.where(ok, s - slope * dist, NEG_BIG)
            m = jnp.max(s, axis=-1, keepdims=True)
            p = jnp.exp(s - m)
            l = jnp.sum(p, axis=-1, keepdims=True)
            acc = jnp.dot(p.astype(BF16), v, preferred_element_type=F32)
            outs.append(acc / l)
            lse_all = jnp.where(lane == head, m + jnp.log(l), lse_all)
        o_ref[:, sl] = jnp.where(lane < HEAD_DIM, outs[0], outs[1]).astype(o_ref.dtype)
    lse_ref[...] = lse_all


def dilated_prompt_branch(q_g, kv_g, slopes_d, g):
    b, d, su, _ = q_g.shape
    tu = DIL_TILE

    def prev(u):
        return jnp.maximum(u - 1, 0)

    blk = (None, None, tu, AW)
    return pl.pallas_call(
        _dilated_prompt_kernel,
        grid=(b, d, su // tu),
        in_specs=[
            pl.BlockSpec(memory_space=pltpu.SMEM),
            pl.BlockSpec(blk, lambda bi, r, u: (bi, r, u, 0)),
            pl.BlockSpec(blk, lambda bi, r, u: (bi, r, prev(u), 0)),
            pl.BlockSpec(blk, lambda bi, r, u: (bi, r, prev(u), 1)),
            pl.BlockSpec(blk, lambda bi, r, u: (bi, r, u, 0)),
            pl.BlockSpec(blk, lambda bi, r, u: (bi, r, u, 1)),
        ],
        out_specs=[
            pl.BlockSpec(blk, lambda bi, r, u: (bi, r, u, 0)),
            pl.BlockSpec((None, None, tu, LANES), lambda bi, r, u: (bi, r, u, 0)),
        ],
        out_shape=[
            jax.ShapeDtypeStruct((b, d, su, AW), BF16),
            jax.ShapeDtypeStruct((b, d, su, LANES), F32),
        ],
        compiler_params=_params("parallel", "parallel", "arbitrary"),
        name=f"dilated_prompt_branch{g}",
    )(slopes_d, q_g, kv_g, kv_g, kv_g, kv_g)


def _merge_proj_residual_kernel(x_ref, o0_ref, o1_ref, o2_ref, l0_ref, l1_ref, l2_ref,
                                u1_ref, u2_ref, u1f_ref, u2f_ref, e_ref, w_ref, y_ref):
    tm = x_ref.shape[0]
    o_refs, l_refs = (o0_ref, o1_ref, o2_ref), (l0_ref, l1_ref, l2_ref)
    undo_b, undo_f = (None, u1_ref, u2_ref), (None, u1f_ref, u2f_ref)
    outs, lses = [], []
    for g in range(N_BRANCH):
        o = o_refs[g][...].reshape(tm, AW)
        lse = l_refs[g][...].reshape(tm, LANES)
        if undo_b[g] is not None:
            o = jnp.dot(undo_b[g][...], o, preferred_element_type=F32)
            lse = jnp.dot(undo_f[g][...], lse, precision=lax.Precision.HIGHEST, preferred_element_type=F32)
        outs.append(o.astype(F32))
        lses.append(lse)
    m = jnp.maximum(jnp.maximum(lses[0], lses[1]), lses[2])
    es = [jnp.exp(l - m) for l in lses]
    inv = 1.0 / (es[0] + es[1] + es[2])
    merged = jnp.zeros((tm, AW), F32)
    for g in range(N_BRANCH):
        w = es[g] * inv
        w_hi = w.astype(BF16)
        w_lo = (w - w_hi.astype(F32)).astype(BF16)
        w_wide = (jnp.dot(w_hi, e_ref[...], preferred_element_type=F32)
                  + jnp.dot(w_lo, e_ref[...], preferred_element_type=F32))
        merged = merged + w_wide * outs[g]
    y_ref[...] = x_ref[...] + jnp.dot(merged.astype(BF16), w_ref[...], preferred_element_type=F32)


def merge_proj_residual(x, outs, lses, w):
    b, s, d = x.shape
    tm = ROW_TILE
    undo = [_deinterleave_matrix(tm, dil).T for dil in DILATIONS[1:]]
    expand = np.zeros((LANES, AW), np.float32)
    expand[np.arange(AW) // HEAD_DIM, np.arange(AW)] = 1.0
    const = lambda shape: pl.BlockSpec(shape, lambda bi, i: (0,) * len(shape))
    grouped = lambda dil, width: pl.BlockSpec((None, dil, tm // dil, width), lambda bi, i: (bi, 0, i, 0))
    return pl.pallas_call(
        _merge_proj_residual_kernel,
        grid=(b, s // tm),
        in_specs=[pl.BlockSpec((None, tm, d), lambda bi, i: (bi, i, 0))]
        + [grouped(dil, AW) for dil in DILATIONS] + [grouped(dil, LANES) for dil in DILATIONS]
        + [const((tm, tm))] * 4 + [const((LANES, AW)), const((AW, d))],
        out_specs=pl.BlockSpec((None, tm, d), lambda bi, i: (bi, i, 0)),
        out_shape=jax.ShapeDtypeStruct((b, s, d), F32),
        compiler_params=_params("parallel", "parallel"),
        name="merge_proj_residual",
    )(x, *outs, *lses, jnp.asarray(undo[0], BF16), jnp.asarray(undo[1], BF16),
      jnp.asarray(undo[0], F32), jnp.asarray(undo[1], F32), jnp.asarray(expand, BF16), w)


def _dilated_sample_kernel(*refs, n_new, write_cache):
    slope_ref, q_ref, kvn_ref = refs[0:3]
    cache_refs = refs[3:6]
    o_ref = refs[6]
    out_refs = refs[7:10] if write_cache else None
    for pp in range(q_ref.shape[0]):
        _dilated_sample_pair(pp, slope_ref, q_ref, kvn_ref, cache_refs, o_ref, out_refs, n_new)


def _dilated_sample_pair(pp, slope_ref, q_ref, kvn_ref, cache_refs, o_ref, out_refs, n_new):
    feat = slice(pp * LANES, (pp + 1) * LANES)
    nr = q_ref.shape[2]
    row = lax.broadcasted_iota(jnp.int32, (nr, LANES), 0)
    lane = lax.broadcasted_iota(jnp.int32, (nr, LANES), 1)
    qmask = jnp.logical_and(lane // HEAD_DIM == row // n_new, row < HEADS_PER_BLOCK * n_new)
    t_col = (row % n_new)[:, 0:1]
    slope = slope_ref[pp, :, 0:1]

    scores, values, news = [], [], []
    for g in range(N_BRANCH):
        d = DILATIONS[g]
        kt = cache_refs[g][0, feat, :]
        vt = cache_refs[g][1, feat, :]
        length = kt.shape[1]
        qf = jnp.where(qmask, q_ref[pp, g], 0.0)
        s = jnp.dot(qf.astype(BF16), kt.astype(BF16), preferred_element_type=F32)
        pos = lax.broadcasted_iota(jnp.int32, (1, length), 1)
        dist = (length + t_col - pos).astype(F32)
        valid = (pos >= t_col) if d == 1 else ((pos & (d - 1)) == t_col)
        scores.append(jnp.where(valid, s - slope * dist, NEG_BIG))
        values.append(vt.astype(BF16))
        for t2 in range(n_new):
            kn = kvn_ref[pp, g, 0, t2:t2 + 1, :].astype(BF16).astype(F32)
            vn = kvn_ref[pp, g, 1, t2:t2 + 1, :].astype(BF16).astype(F32)
            sn = jnp.sum(qf * kn, axis=-1, keepdims=True)
            ok = (t_col >= t2) if d == 1 else (t_col == t2)
            sn = jnp.where(ok, sn - slope * ((t_col - t2) * d).astype(F32), NEG_BIG)
            news.append((sn, vn))
        if out_refs is not None:
            for c in range(2):
                src = cache_refs[g][c, feat, :]
                new8 = kvn_ref[pp, g, c]
                padded = jnp.concatenate([new8, jnp.zeros((LANES - new8.shape[0], LANES), F32)], axis=0)
                new_cols = pltpu.roll(padded.T, LANES - n_new, axis=1)
                shifted = pltpu.roll(src, length - n_new, axis=1)
                tail_lane = lax.broadcasted_iota(jnp.int32, (LANES, LANES), 1)
                tail = jnp.where(tail_lane >= LANES - n_new, new_cols, shifted[:, length - LANES:])
                if length > LANES:
                    out_refs[g][c, feat, 0:length - LANES] = shifted[:, 0:length - LANES]
                out_refs[g][c, feat, length - LANES:] = tail

    m = jnp.max(scores[0], axis=-1, keepdims=True)
    for s in scores[1:]:
        m = jnp.maximum(m, jnp.max(s, axis=-1, keepdims=True))
    for sn, _ in news:
        m = jnp.maximum(m, sn)
    l = jnp.zeros((nr, 1), F32)
    acc = jnp.zeros((nr, LANES), F32)
    for s, vt in zip(scores, values):
        p = jnp.exp(s - m)
        l = l + jnp.sum(p, axis=-1, keepdims=True)
        acc = acc + _dot_nt(p.astype(BF16), vt)
    for sn, vn in news:
        p = jnp.exp(sn - m)
        l = l + p
        acc = acc + p.astype(BF16).astype(F32) * vn
    out = acc / l
    second = pltpu.roll(out, nr - n_new, axis=0)
    o_ref[pp] = jnp.where(lane < HEAD_DIM, out, second)[0:o_ref.shape[1], :]


def dilated_sample_attention(q16, kvn, slope_rows, caches_t, *, n_new, write_cache):
    db = q16.shape[0]
    nr = q16.shape[3]
    npp = SAMPLE_PAIRS_PER_STEP
    in_specs = [
        pl.BlockSpec((npp, nr, LANES), lambda b, hp: (hp, 0, 0)),
        pl.BlockSpec((None, npp, N_BRANCH, nr, LANES), lambda b, hp: (b, hp, 0, 0, 0)),
        pl.BlockSpec((None, npp, N_BRANCH, 2, 8, LANES), lambda b, hp: (b, hp, 0, 0, 0, 0)),
    ]
    cache_specs = [pl.BlockSpec((None, 2, npp * LANES, c.shape[3]), lambda b, hp: (b, 0, hp, 0)) for c in caches_t]
    cache_args = list(caches_t)
    out_specs = [pl.BlockSpec((None, npp, 8, LANES), lambda b, hp: (b, hp, 0, 0))]
    out_shape = [jax.ShapeDtypeStruct((db, N_PAIRS, 8, LANES), F32)]
    if write_cache:
        out_specs += cache_specs
        out_shape += [jax.ShapeDtypeStruct(c.shape, F32) for c in cache_args]
    outs = pl.pallas_call(
        functools.partial(_dilated_sample_kernel, n_new=n_new, write_cache=write_cache),
        grid=(db, N_PAIRS // npp),
        in_specs=in_specs + cache_specs,
        out_specs=out_specs,
        out_shape=out_shape,
        compiler_params=_params("parallel", "arbitrary"),
        name="dilated_sample_attention" + ("_shift" if write_cache else ""),
    )(slope_rows, q16, kvn, *cache_args)
    return outs


def _proj_residual_kernel(x_ref, o_ref, w_ref, y_ref):
    y_ref[...] = x_ref[...] + jnp.dot(o_ref[...].astype(BF16), w_ref[...], preferred_element_type=F32)


def proj_residual(x, o, w):
    rows, d = x.shape
    tm = min(ROW_TILE, rows)
    return pl.pallas_call(
        _proj_residual_kernel,
        grid=(rows // tm,),
        in_specs=[
            pl.BlockSpec((tm, d), lambda i: (i, 0)),
            pl.BlockSpec((tm, AW), lambda i: (i, 0)),
            pl.BlockSpec((AW, d), lambda i: (0, 0)),
        ],
        out_specs=pl.BlockSpec((tm, d), lambda i: (i, 0)),
        out_shape=jax.ShapeDtypeStruct((rows, d), F32),
        compiler_params=_params("parallel"),
        name="proj_residual",
    )(x, o, w)


def _ffn_kernel(x_ref, g_ref, wg_ref, wu_ref, wd_ref, y_ref, h_ref, acc_ref):
    j = pl.program_id(1)

    @pl.when(j == 0)
    def _():
        h_ref[...] = _rmsnorm_bf16(x_ref[...], g_ref[...])
        acc_ref[...] = jnp.zeros_like(acc_ref)

    h = h_ref[...]
    gate = jnp.dot(h, wg_ref[...], preferred_element_type=F32)
    up = jnp.dot(h, wu_ref[...], preferred_element_type=F32)
    act = gate * (1.0 / (1.0 + jnp.exp(-gate))) * up
    acc_ref[...] += jnp.dot(act.astype(BF16), wd_ref[...], preferred_element_type=F32)

    @pl.when(j == pl.num_programs(1) - 1)
    def _():
        y_ref[...] = x_ref[...] + acc_ref[...]


def ffn_residual(x, g, w_gate_up, w_down, *, tm=1024, tf=256):
    rows, d = x.shape
    f = w_down.shape[0]
    tm = min(tm, rows)
    nf = f // tf
    return pl.pallas_call(
        _ffn_kernel,
        grid=(rows // tm, nf),
        in_specs=[
            pl.BlockSpec((tm, d), lambda i, j: (i, 0)),
            pl.BlockSpec((1, d), lambda i, j: (0, 0)),
            pl.BlockSpec((d, tf), lambda i, j: (0, j)),
            pl.BlockSpec((d, tf), lambda i, j: (0, nf + j)),
            pl.BlockSpec((tf, d), lambda i, j: (j, 0)),
        ],
        out_specs=pl.BlockSpec((tm, d), lambda i, j: (i, 0)),
        out_shape=jax.ShapeDtypeStruct((rows, d), F32),
        scratch_shapes=[pltpu.VMEM((tm, d), BF16), pltpu.VMEM((tm, d), F32)],
        compiler_params=_params("parallel", "arbitrary"),
        name="ffn_residual",
    )(x, g.reshape(1, d), w_gate_up, w_gate_up, w_down)


def _rmsnorm_kernel(x_ref, g_ref, y_ref):
    x = x_ref[...]
    y = x * lax.rsqrt(jnp.mean(x * x, axis=-1, keepdims=True) + RMS_EPS)
    y_ref[...] = y * g_ref[...]


def rmsnorm(x, g):
    rows, d = x.shape
    tm = min(ROW_TILE, rows)
    return pl.pallas_call(
        _rmsnorm_kernel,
        grid=(rows // tm,),
        in_specs=[pl.BlockSpec((tm, d), lambda i: (i, 0)), pl.BlockSpec((1, d), lambda i: (0, 0))],
        out_specs=pl.BlockSpec((tm, d), lambda i: (i, 0)),
        out_shape=jax.ShapeDtypeStruct((rows, d), F32),
        compiler_params=_params("parallel"),
        name="rmsnorm",
    )(x, g.reshape(1, d))


def _to_token_major(kv_t, lead):
    n = len(lead)
    x = kv_t.reshape(*lead, 2, N_HEADS, HEAD_DIM, kv_t.shape[-1])
    return x.transpose(*range(n), n + 3, n, n + 1, n + 2)


def kernel(x_prompt, x_sample, cache_fox_kv, cache_fox_logf, cache_win0, cache_win1, cache_win2, page_table,
           norm_g, w_qkvf, b_f, wo_a, norm_kv, w_kv_shared, w_q_b, wo_b, w_gate_up, w_down, norm_final):
    b, s, d = x_prompt.shape
    db, t, _ = x_sample.shape
    n_a = w_qkvf.shape[0]
    depth = norm_g.shape[0]
    n_pool, page = cache_fox_kv.shape[1], cache_fox_kv.shape[2]
    caches = (cache_win0, cache_win1, cache_win2)
    assert d == AW and s % (DILATIONS[-1] * DIL_TILE) == 0 and s % ROW_TILE == 0 and t <= DILATIONS[1]
    assert all(c.shape[1] == w for c, w in zip(caches, WINDOWS)) and WINDOWS[-1] <= s
    assert 2 * t <= 8 and ROW_TILE % (DILATIONS[-1] * 16) == 0

    slopes = jnp.exp2(-8.0 * jnp.arange(1, N_HEADS + 1, dtype=F32) / N_HEADS)
    cache_kvt = cache_fox_kv.transpose(0, 1, 3, 4, 5, 2).reshape(n_a, n_pool, 2 * AW, page)
    cache_lft = cache_fox_logf.transpose(0, 1, 3, 2)
    caches_t = [c.transpose(0, 2, 3, 4, 1).reshape(db, 2, AW, c.shape[1]) for c in caches]
    pt_flat = page_table.reshape(-1)
    slope_rows = jnp.pad(jnp.repeat(slopes.reshape(N_PAIRS, HEADS_PER_BLOCK), t, axis=1),
                         ((0, 0), (0, 16 - HEADS_PER_BLOCK * t)))
    slope_rows = jnp.broadcast_to(slope_rows[:, :, None], (N_PAIRS, 16, LANES))

    xp = x_prompt
    xs = x_sample.reshape(db * t, d)
    fox_kv_p = jnp.zeros((n_a, b, 2 * AW, s), F32)
    fox_lf_p, fox_kv_s, fox_lf_s = [], [], []
    tq = ROW_TILE
    for layer in range(depth):
        g1, g2 = norm_g[layer, 0], norm_g[layer, 1]
        if layer < n_a:
            a = layer
            wt = w_qkvf[a].T
            wt_qkv = wt[:3 * AW].astype(BF16)
            wt_f = jnp.pad(wt[3 * AW:], ((0, LANES - N_HEADS), (0, 0))).astype(BF16)
            wo = wo_a[a].astype(BF16)
            q, fox_kv_p, kvb, lf_t, qn, kn = fox_project_prompt(xp, g1, wt_qkv, wt_f, b_f[a], fox_kv_p, a)
            c, cmax, cmin = cumsum_seq(lf_t)
            first = first_needed_tile(qn[:, :, 0, :N_HEADS], kn[:, :, :, 0], cmax[:, :, :, 0], cmin[:, :, :, 0])
            c = c.reshape(b, N_PAIRS, HEADS_PER_BLOCK, s)
            c_cols = c.transpose(0, 1, 3, 2)
            c_rows = c.reshape(b, N_PAIRS, HEADS_PER_BLOCK, s // tq, tq).transpose(0, 1, 3, 2, 4)
            o = fox_prompt_attention(first, q, kvb, c_cols, c_rows)
            xp = proj_residual(xp.reshape(b * s, d), o.reshape(b * s, AW), wo).reshape(b, s, d)
            fox_lf_p.append(lf_t.transpose(0, 2, 1))
            q, kv, lf = fox_project_sample(xs, g1, wt_qkv, wt_f, b_f[a])
            o = fox_sample_attention(pt_flat, q.astype(F32).reshape(db, t, AW), kv.reshape(db, t, 2 * AW),
                                     lf.reshape(db, t, N_HEADS), cache_kvt, cache_lft, a)
            xs = proj_residual(xs, o.reshape(db * t, AW), wo)
            fox_kv_s.append(kv.reshape(db, t, 2, N_HEADS, HEAD_DIM))
            fox_lf_s.append(lf.reshape(db, t, N_HEADS))
        else:
            bl = layer - n_a
            wq = w_q_b[bl].astype(BF16)
            wo = wo_b[bl].astype(BF16)
            outs, lses = [], []
            for g in range(N_BRANCH):
                q_g = norm_matmul_grouped(xp, g1, wq[:, g * AW:(g + 1) * AW], DILATIONS[g], scale=Q_SCALE)
                o_g, lse_g = dilated_prompt_branch(q_g, kvsh_p[g], slopes * float(DILATIONS[g]), g)
                outs.append(o_g)
                lses.append(lse_g)
            xp = merge_proj_residual(xp, outs, lses, wo)
            q3 = norm_matmul(xs, g1, wq, BF16, scale=Q_SCALE).astype(F32)
            q16 = q3.reshape(db, t, N_BRANCH, N_PAIRS, LANES).transpose(0, 3, 2, 1, 4)
            q16 = jnp.concatenate([q16] * HEADS_PER_BLOCK, axis=3)
            q16 = jnp.pad(q16, ((0, 0),) * 3 + ((0, 16 - HEADS_PER_BLOCK * t), (0, 0)))
            res = dilated_sample_attention(q16, kvn, slope_rows, caches_t, n_new=t, write_cache=(bl == 0))
            if bl == 0:
                win_s = [_to_token_major(res[1 + g], (db,)) for g in range(N_BRANCH)]
            o = res[0][:, :, :t].transpose(0, 2, 1, 3).reshape(db * t, AW)
            xs = proj_residual(xs, o, wo)
        wgu = w_gate_up[layer].astype(BF16)
        wd = w_down[layer].astype(BF16)
        xp = ffn_residual(xp.reshape(b * s, d), g2, wgu, wd).reshape(b, s, d)
        xs = ffn_residual(xs, g2, wgu, wd)
        if layer == n_a - 1:
            wkv = w_kv_shared.astype(BF16)
            kvsh_p = [norm_matmul_grouped(xp, norm_kv, wkv[:, g * 2 * AW:(g + 1) * 2 * AW], DILATIONS[g])
                      for g in range(N_BRANCH)]
            tail = WINDOWS[-1]
            x_tail = xp[:, s - tail:].reshape(b * tail, d)
            kv_tail = norm_matmul(x_tail, norm_kv, wkv, F32).reshape(b, tail, N_BRANCH, 2, N_HEADS, HEAD_DIM)
            win_p = [kv_tail[:, tail - WINDOWS[g]:, g] for g in range(N_BRANCH)]
            kvsh_s = norm_matmul(xs, norm_kv, wkv, F32)
            kvn = kvsh_s.reshape(db, t, N_BRANCH, 2, N_PAIRS, LANES).transpose(0, 4, 2, 3, 1, 5)
            kvn = jnp.pad(kvn, ((0, 0),) * 4 + ((0, 8 - t), (0, 0)))
    y_prompt = rmsnorm(xp.reshape(b * s, d), norm_final).reshape(b, s, d)
    y_sample = rmsnorm(xs, norm_final).reshape(db, t, d)
    return (y_prompt, y_sample, _to_token_major(fox_kv_p, (n_a, b)), jnp.stack(fox_lf_p), jnp.stack(fox_kv_s),
            jnp.stack(fox_lf_s), win_p[0], win_s[0], win_p[1], win_s[1], win_p[2], win_s[2])
```

```python
import functools

import numpy as np
import jax
import jax.numpy as jnp
from jax import lax
from jax.experimental import pallas as pl
from jax.experimental.pallas import tpu as pltpu

F32 = jnp.float32
BF16 = jnp.bfloat16

N_HEADS = 16
HEAD_DIM = 64
AW = N_HEADS * HEAD_DIM
N_BRANCH = 3
WINDOWS = (128, 512, 2048)
DILATIONS = (1, 4, 16)
N_TAPS = 129
RMS_EPS = 1e-6
NEG_BIG = -1e30
LANES = 128
HEADS_PER_BLOCK = LANES // HEAD_DIM
N_PAIRS = N_HEADS // HEADS_PER_BLOCK
Q_SCALE = HEAD_DIM ** -0.5
VMEM_LIMIT = 56 * 1024 * 1024
ROW_TILE = 512
FOX_SUB = 512
DIL_TILE = 128
FOX_PAGES_PER_STEP = 8
SAMPLE_PAIRS_PER_STEP = 2
SKIP_LOGIT = 120.0

_NT = (((1,), (1,)), ((), ()))


def _params(*sem, flags=None):
    return pltpu.CompilerParams(dimension_semantics=sem, vmem_limit_bytes=VMEM_LIMIT, flags=flags)


def _rmsnorm_bf16(x, g):
    y = x * lax.rsqrt(jnp.mean(x * x, axis=-1, keepdims=True) + RMS_EPS)
    return (y * g).astype(BF16)


def _log_sigmoid(x):
    return jnp.minimum(x, 0.0) - jnp.log1p(jnp.exp(-jnp.abs(x)))


def _select_head(q, hh):
    lane = lax.broadcasted_iota(jnp.int32, (1, LANES), 1)
    keep = ((lane // HEAD_DIM) == hh).astype(F32)
    return (q.astype(F32) * keep).astype(BF16)


def _dot_nt(a, b):
    return lax.dot_general(a, b, _NT, preferred_element_type=F32)


def _deinterleave_matrix(n, d):
    p = np.zeros((n, n), np.float32)
    i = np.arange(n)
    p[(i % d) * (n // d) + i // d, i] = 1.0
    return p


def _norm_matmul_kernel(x_ref, g_ref, w_ref, o_ref, h_ref, *, scale):
    @pl.when(pl.program_id(1) == 0)
    def _():
        h_ref[...] = _rmsnorm_bf16(x_ref[...], g_ref[...])

    acc = jnp.dot(h_ref[...], w_ref[...], preferred_element_type=F32)
    o_ref[...] = (acc * scale).astype(o_ref.dtype)


def norm_matmul(x, g, w, out_dtype, *, scale=1.0, tn=1024):
    rows, d = x.shape
    n = w.shape[1]
    tm = min(ROW_TILE, rows)
    return pl.pallas_call(
        functools.partial(_norm_matmul_kernel, scale=scale),
        grid=(rows // tm, n // tn),
        in_specs=[
            pl.BlockSpec((tm, d), lambda i, j: (i, 0)),
            pl.BlockSpec((1, d), lambda i, j: (0, 0)),
            pl.BlockSpec((d, tn), lambda i, j: (0, j)),
        ],
        out_specs=pl.BlockSpec((tm, tn), lambda i, j: (i, j)),
        out_shape=jax.ShapeDtypeStruct((rows, n), out_dtype),
        scratch_shapes=[pltpu.VMEM((tm, d), BF16)],
        compiler_params=_params("parallel", "arbitrary"),
        name="norm_matmul",
    )(x, g.reshape(1, d), w)


def _norm_matmul_t_kernel(x_ref, g_ref, wt_ref, o_ref, h_ref):
    @pl.when(pl.program_id(2) == 0)
    def _():
        h_ref[...] = _rmsnorm_bf16(x_ref[...], g_ref[...])

    o_ref[...] = _dot_nt(wt_ref[...], h_ref[...])


def norm_matmul_t(x, g, w_t, *, tn=1024):
    b, s, d = x.shape
    n = w_t.shape[0]
    tm = min(ROW_TILE, s)
    return pl.pallas_call(
        _norm_matmul_t_kernel,
        grid=(b, s // tm, n // tn),
        in_specs=[
            pl.BlockSpec((None, tm, d), lambda bi, i, j: (bi, i, 0)),
            pl.BlockSpec((1, d), lambda bi, i, j: (0, 0)),
            pl.BlockSpec((tn, d), lambda bi, i, j: (j, 0)),
        ],
        out_specs=pl.BlockSpec((None, tn, tm), lambda bi, i, j: (bi, j, i)),
        out_shape=jax.ShapeDtypeStruct((b, n, s), F32),
        scratch_shapes=[pltpu.VMEM((tm, d), BF16)],
        compiler_params=_params("parallel", "parallel", "arbitrary"),
        name="norm_matmul_t",
    )(x, g.reshape(1, d), w_t)


def _norm_matmul_grouped_kernel(x_ref, g_ref, p_ref, w_ref, o_ref, h_ref, *, scale, dil):
    @pl.when(pl.program_id(2) == 0)
    def _():
        h = _rmsnorm_bf16(x_ref[...], g_ref[...])
        if dil > 1:
            h = jnp.dot(p_ref[...], h, preferred_element_type=F32).astype(BF16)
        h_ref[...] = h

    acc = jnp.dot(h_ref[...], w_ref[...], preferred_element_type=F32) * scale
    o_ref[...] = acc.reshape(o_ref.shape).astype(o_ref.dtype)


def norm_matmul_grouped(x, g, w, dil, *, scale=1.0, tn=1024):
    b, s, d = x.shape
    n = w.shape[1]
    tm = ROW_TILE
    perm = jnp.asarray(_deinterleave_matrix(tm, dil), BF16)
    return pl.pallas_call(
        functools.partial(_norm_matmul_grouped_kernel, scale=scale, dil=dil),
        grid=(b, s // tm, n // tn),
        in_specs=[
            pl.BlockSpec((None, tm, d), lambda bi, i, j: (bi, i, 0)),
            pl.BlockSpec((1, d), lambda bi, i, j: (0, 0)),
            pl.BlockSpec((tm, tm), lambda bi, i, j: (0, 0)),
            pl.BlockSpec((d, tn), lambda bi, i, j: (0, j)),
        ],
        out_specs=pl.BlockSpec((None, dil, tm // dil, tn), lambda bi, i, j: (bi, 0, i, j)),
        out_shape=jax.ShapeDtypeStruct((b, dil, s // dil, n), BF16),
        scratch_shapes=[pltpu.VMEM((tm, d), BF16)],
        compiler_params=_params("parallel", "parallel", "arbitrary"),
        name=f"norm_matmul_grouped{dil}",
    )(x, g.reshape(1, d), perm, w)


def _fox_proj_prompt_kernel(x_ref, g_ref, wt_ref, wft_ref, bf_ref, seg_ref, kv_in_ref,
                            q_ref, kv_ref, kvb_ref, lf_ref, qn_ref, kn_ref):
    del kv_in_ref
    tm = x_ref.shape[0]
    h = _rmsnorm_bf16(x_ref[...], g_ref[...])
    q = (_dot_nt(h, wt_ref[0:AW, :]) * Q_SCALE).astype(BF16)
    q_ref[...] = q
    qf = q.astype(F32)
    sq = qf * qf
    sq_hi = sq.astype(BF16)
    sq_lo = (sq - sq_hi.astype(F32)).astype(BF16)
    q_sq = (jnp.dot(sq_hi, seg_ref[...], preferred_element_type=F32)
            + jnp.dot(sq_lo, seg_ref[...], preferred_element_type=F32))
    qn_ref[...] = jnp.broadcast_to(jnp.sqrt(jnp.max(q_sq, axis=0, keepdims=True)), qn_ref.shape)
    for c in range(2):
        rows = slice(c * AW, (c + 1) * AW)
        kv = _dot_nt(wt_ref[AW + c * AW:2 * AW + c * AW, :], h)
        kv_ref[rows, :] = kv
        kvb = kv.astype(BF16)
        kvb_ref[rows, :] = kvb
        if c == 0:
            kf = kvb.astype(F32)
            k_sq = jnp.sum((kf * kf).reshape(N_HEADS, HEAD_DIM, tm), axis=1)
            kn_ref[...] = jnp.broadcast_to(jnp.sqrt(jnp.max(k_sq, axis=1, keepdims=True)), kn_ref.shape)
    f = _dot_nt(wft_ref[...], h)[0:N_HEADS, :]
    lf_ref[...] = _log_sigmoid(f + bf_ref[...])


def fox_project_prompt(x, g, wt_qkv, wt_f, b_f, kv_out, layer):
    b, s, d = x.shape
    tm = ROW_TILE
    nt = s // tm
    seg = np.zeros((AW, LANES), np.float32)
    seg[np.arange(AW), np.arange(AW) // HEAD_DIM] = 1.0
    return pl.pallas_call(
        _fox_proj_prompt_kernel,
        grid=(b, nt),
        in_specs=[
            pl.BlockSpec((None, tm, d), lambda bi, i: (bi, i, 0)),
            pl.BlockSpec((1, d), lambda bi, i: (0, 0)),
            pl.BlockSpec((3 * AW, d), lambda bi, i: (0, 0)),
            pl.BlockSpec((LANES, d), lambda bi, i: (0, 0)),
            pl.BlockSpec((N_HEADS, 1), lambda bi, i: (0, 0)),
            pl.BlockSpec((AW, LANES), lambda bi, i: (0, 0)),
            pl.BlockSpec(memory_space=pl.ANY),
        ],
        out_specs=[
            pl.BlockSpec((None, tm, AW), lambda bi, i: (bi, i, 0)),
            pl.BlockSpec((None, None, 2 * AW, tm), lambda bi, i: (layer, bi, 0, i)),
            pl.BlockSpec((None, None, 2 * AW, tm), lambda bi, i: (bi, i, 0, 0)),
            pl.BlockSpec((None, N_HEADS, tm), lambda bi, i: (bi, 0, i)),
            pl.BlockSpec((None, None, 8, LANES), lambda bi, i: (bi, i, 0, 0)),
            pl.BlockSpec((None, None, N_HEADS, LANES), lambda bi, i: (bi, i, 0, 0)),
        ],
        out_shape=[
            jax.ShapeDtypeStruct((b, s, AW), BF16),
            jax.ShapeDtypeStruct(kv_out.shape, F32),
            jax.ShapeDtypeStruct((b, nt, 2 * AW, tm), BF16),
            jax.ShapeDtypeStruct((b, N_HEADS, s), F32),
            jax.ShapeDtypeStruct((b, nt, 8, LANES), F32),
            jax.ShapeDtypeStruct((b, nt, N_HEADS, LANES), F32),
        ],
        input_output_aliases={6: 1},
        compiler_params=_params("parallel", "parallel"),
        name="fox_project_prompt",
    )(x, g.reshape(1, d), wt_qkv, wt_f, b_f.reshape(N_HEADS, 1), jnp.asarray(seg, BF16), kv_out)


def _fox_proj_sample_kernel(x_ref, g_ref, wt_ref, wft_ref, bf_ref, q_ref, kv_ref, lf_ref):
    h = _rmsnorm_bf16(x_ref[...], g_ref[...])
    q_ref[...] = (_dot_nt(h, wt_ref[0:AW, :]) * Q_SCALE).astype(BF16)
    kv_ref[...] = _dot_nt(h, wt_ref[AW:3 * AW, :])
    f = _dot_nt(h, wft_ref[...])[:, 0:N_HEADS]
    lf_ref[...] = _log_sigmoid(f + bf_ref[...])


def fox_project_sample(x, g, wt_qkv, wt_f, b_f):
    rows, d = x.shape
    return pl.pallas_call(
        _fox_proj_sample_kernel,
        grid=(1,),
        in_specs=[
            pl.BlockSpec((rows, d), lambda i: (0, 0)),
            pl.BlockSpec((1, d), lambda i: (0, 0)),
            pl.BlockSpec((3 * AW, d), lambda i: (0, 0)),
            pl.BlockSpec((LANES, d), lambda i: (0, 0)),
            pl.BlockSpec((1, N_HEADS), lambda i: (0, 0)),
        ],
        out_specs=[
            pl.BlockSpec((rows, AW), lambda i: (0, 0)),
            pl.BlockSpec((rows, 2 * AW), lambda i: (0, 0)),
            pl.BlockSpec((rows, N_HEADS), lambda i: (0, 0)),
        ],
        out_shape=[
            jax.ShapeDtypeStruct((rows, AW), BF16),
            jax.ShapeDtypeStruct((rows, 2 * AW), F32),
            jax.ShapeDtypeStruct((rows, N_HEADS), F32),
        ],
        compiler_params=_params("arbitrary"),
        name="fox_project_sample",
    )(x, g.reshape(1, d), wt_qkv, wt_f, b_f.reshape(1, N_HEADS))


def _cumsum_kernel(lf_ref, c_ref, cmax_ref, cmin_ref, carry_ref):
    @pl.when(pl.program_id(1) == 0)
    def _():
        carry_ref[...] = jnp.zeros_like(carry_ref)

    ts = lf_ref.shape[1]
    src = lax.broadcasted_iota(jnp.int32, (ts, ts), 0)
    dst = lax.broadcasted_iota(jnp.int32, (ts, ts), 1)
    upper = (src <= dst).astype(F32)
    c = jnp.dot(lf_ref[...], upper, precision=lax.Precision.HIGHEST, preferred_element_type=F32)
    c = c + carry_ref[:, 0:1]
    c_ref[...] = c
    carry_ref[...] = jnp.broadcast_to(c[:, ts - 1:ts], carry_ref.shape)
    cmax_ref[...] = jnp.broadcast_to(jnp.max(c, axis=1, keepdims=True), cmax_ref.shape)
    cmin_ref[...] = jnp.broadcast_to(jnp.min(c, axis=1, keepdims=True), cmin_ref.shape)


def cumsum_seq(lf_t):
    b, h, s = lf_t.shape
    ts = ROW_TILE
    stat = pl.BlockSpec((None, None, h, LANES), lambda bi, i: (bi, i, 0, 0))
    return pl.pallas_call(
        _cumsum_kernel,
        grid=(b, s // ts),
        in_specs=[pl.BlockSpec((None, h, ts), lambda bi, i: (bi, 0, i))],
        out_specs=[pl.BlockSpec((None, h, ts), lambda bi, i: (bi, 0, i)), stat, stat],
        out_shape=[jax.ShapeDtypeStruct((b, h, s), F32)]
        + [jax.ShapeDtypeStruct((b, s // ts, h, LANES), F32)] * 2,
        scratch_shapes=[pltpu.VMEM((h, LANES), F32)],
        compiler_params=_params("parallel", "arbitrary"),
        name="cumsum_seq",
    )(lf_t)


def _fox_prompt_kernel(first_ref, q_ref, kt_ref, vt_ref, cq_ref, ck_ref, o_ref, *, tq, rq):
    i = pl.program_id(2)
    first = first_ref[(pl.program_id(0) * pl.num_programs(1) + pl.program_id(1)) * pl.num_programs(2) + i]
    nsub = tq // rq
    q = q_ref[...]
    lane = lax.broadcasted_iota(jnp.int32, (1, LANES), 1)
    qm = [_select_head(q, hh) for hh in range(HEADS_PER_BLOCK)]
    cq = [cq_ref[:, hh:hh + 1] for hh in range(HEADS_PER_BLOCK)]
    ones = jnp.ones((HEAD_DIM, tq), BF16)

    def tile(j, carry, masked):
        kt = kt_ref[j]
        vt = vt_ref[j]
        vts = [jnp.concatenate([vt[0:HEAD_DIM], ones], axis=0), jnp.concatenate([ones, vt[HEAD_DIM:]], axis=0)]
        new = []
        for hh in range(HEADS_PER_BLOCK):
            ck = ck_ref[j, hh:hh + 1, :]
            subs = []
            for sb in range(nsub):
                m, acc = carry[hh][sb]
                rows = slice(sb * rq, (sb + 1) * rq)
                s = jnp.dot(qm[hh][rows], kt, preferred_element_type=F32) + (cq[hh][rows] - ck)
                if masked:
                    row = lax.broadcasted_iota(jnp.int32, (rq, tq), 0) + sb * rq
                    col = lax.broadcasted_iota(jnp.int32, (rq, tq), 1)
                    s = jnp.where(col <= row, s, NEG_BIG)
                m_new = jnp.maximum(m, jnp.max(s, axis=-1, keepdims=True))
                p = jnp.exp(s - m_new)
                acc = jnp.exp(m - m_new) * acc + _dot_nt(p.astype(BF16), vts[hh])
                subs.append((m_new, acc))
            new.append(tuple(subs))
        return tuple(new)

    init = tuple(tuple((jnp.full((rq, 1), NEG_BIG, F32), jnp.zeros((rq, LANES), F32)) for _ in range(nsub))
                 for _ in range(HEADS_PER_BLOCK))
    carry = lax.fori_loop(first, i, lambda j, c: tile(j, c, False), init)
    carry = tile(i, carry, True)
    for sb in range(nsub):
        acc0, acc1 = carry[0][sb][1], carry[1][sb][1]
        out = jnp.where(lane < HEAD_DIM, acc0 / acc0[:, HEAD_DIM:HEAD_DIM + 1], acc1 / acc1[:, 0:1])
        o_ref[sb * rq:(sb + 1) * rq, :] = out.astype(o_ref.dtype)


def first_needed_tile(qn, kn, cmax, cmin):
    nt = qn.shape[1]
    bound = (qn[:, :, None, :] * (kn[:, None, :, :] + kn[:, :, None, :])
             + cmax[:, :, None, :] - cmin[:, None, :, :])
    needed = bound >= -SKIP_LOGIT
    needed = jnp.any(needed.reshape(needed.shape[:3] + (N_PAIRS, HEADS_PER_BLOCK)), axis=-1)
    tile_j = jnp.arange(nt, dtype=jnp.int32)[None, None, :, None]
    tile_i = jnp.arange(nt, dtype=jnp.int32)[None, :, None, None]
    first = jnp.min(jnp.where(jnp.logical_and(needed, tile_j < tile_i), tile_j, tile_i), axis=2)
    return first.transpose(0, 2, 1).reshape(-1).astype(jnp.int32)


def fox_prompt_attention(first, q, kvb, c_cols, c_rows):
    b, s, _ = q.shape
    tq = ROW_TILE
    nt = s // tq
    grid_spec = pltpu.PrefetchScalarGridSpec(
        num_scalar_prefetch=1,
        grid=(b, N_PAIRS, nt),
        in_specs=[
            pl.BlockSpec((None, tq, LANES), lambda bi, hp, i, f: (bi, i, hp)),
            pl.BlockSpec((None, nt, LANES, tq), lambda bi, hp, i, f: (bi, 0, hp, 0)),
            pl.BlockSpec((None, nt, LANES, tq), lambda bi, hp, i, f: (bi, 0, N_PAIRS + hp, 0)),
            pl.BlockSpec((None, None, tq, HEADS_PER_BLOCK), lambda bi, hp, i, f: (bi, hp, i, 0)),
            pl.BlockSpec((None, None, nt, HEADS_PER_BLOCK, tq), lambda bi, hp, i, f: (bi, hp, 0, 0, 0)),
        ],
        out_specs=pl.BlockSpec((None, tq, LANES), lambda bi, hp, i, f: (bi, i, hp)),
    )
    return pl.pallas_call(
        functools.partial(_fox_prompt_kernel, tq=tq, rq=FOX_SUB),
        grid_spec=grid_spec,
        out_shape=jax.ShapeDtypeStruct((b, s, AW), BF16),
        compiler_params=_params("parallel", "parallel", "arbitrary"),
        name="fox_prompt_attention",
    )(first, q, kvb, kvb, c_cols, c_rows)


def _head_mask(rows):
    head = lax.broadcasted_iota(jnp.int32, (rows, AW), 1) // HEAD_DIM
    return head == lax.broadcasted_iota(jnp.int32, (rows, AW), 0)


def _row_to_col(row):
    eye = (lax.broadcasted_iota(jnp.int32, (N_HEADS, N_HEADS), 0)
           == lax.broadcasted_iota(jnp.int32, (N_HEADS, N_HEADS), 1))
    return jnp.sum(jnp.where(eye, jnp.broadcast_to(row, (N_HEADS, N_HEADS)), 0.0), axis=1, keepdims=True)


def _fox_sample_kernel(pt_ref, q_ref, kvn_ref, lfn_ref, *refs, n_new, pps):
    del pt_ref
    kv_refs, lf_refs, o_ref = refs[0:pps], refs[pps:2 * pps], refs[2 * pps]
    qbd_ref, qbdb_ref, cn_ref, m_ref, l_ref, acc_ref, carry_ref = refs[2 * pps + 1:]
    p = pl.program_id(1)
    nrow = n_new * N_HEADS
    mask16 = _head_mask(N_HEADS)

    @pl.when(p == 0)
    def _():
        run = jnp.zeros((1, N_HEADS), F32)
        for t in range(n_new):
            rows = slice(t * N_HEADS, (t + 1) * N_HEADS)
            qt = jnp.where(mask16, jnp.broadcast_to(q_ref[t:t + 1, :], (N_HEADS, AW)), 0.0)
            qbd_ref[rows, :] = qt
            qbdb_ref[rows, :] = qt.astype(BF16)
            run = run + lfn_ref[t:t + 1, :]
            cn_ref[rows, :] = jnp.broadcast_to(_row_to_col(run), (N_HEADS, LANES))
        m_ref[...] = jnp.full_like(m_ref, NEG_BIG)
        l_ref[...] = jnp.zeros_like(l_ref)
        acc_ref[...] = jnp.zeros_like(acc_ref)
        carry_ref[...] = jnp.zeros_like(carry_ref)

    page = lf_refs[0].shape[1]
    src = lax.broadcasted_iota(jnp.int32, (page, page), 0)
    dst = lax.broadcasted_iota(jnp.int32, (page, page), 1)
    later = (src > dst).astype(F32)
    cn = cn_ref[:, 0:1]
    qbd_b = qbdb_ref[...]
    carry = carry_ref[:, 0:1]
    scores, values = [], []
    for k in range(pps):
        lf_t = lf_refs[k][...]
        suffix = jnp.dot(lf_t, later, precision=lax.Precision.HIGHEST, preferred_element_type=F32) + carry
        carry = suffix[:, 0:1] + lf_t[:, 0:1]
        kt = kv_refs[k][0:AW, :].astype(BF16)
        s = jnp.dot(qbd_b, kt, preferred_element_type=F32)
        scores.append(s + (jnp.concatenate([suffix] * n_new, axis=0) + cn))
        values.append(kv_refs[k][AW:2 * AW, :].astype(BF16))
    carry_ref[...] = jnp.broadcast_to(carry, carry_ref.shape)
    m_old = m_ref[:, 0:1]
    m_new = m_old
    for s in scores:
        m_new = jnp.maximum(m_new, jnp.max(s, axis=-1, keepdims=True))
    alpha = jnp.exp(m_old - m_new)
    l_new = alpha * l_ref[:, 0:1]
    acc_new = alpha * acc_ref[...]
    for s, vt in zip(scores, values):
        pr = jnp.exp(s - m_new)
        l_new = l_new + jnp.sum(pr, axis=-1, keepdims=True)
        acc_new = acc_new + _dot_nt(pr.astype(BF16), vt)
    m_ref[...] = jnp.broadcast_to(m_new, m_ref.shape)
    l_ref[...] = jnp.broadcast_to(l_new, l_ref.shape)
    acc_ref[...] = acc_new

    @pl.when(p == pl.num_programs(1) - 1)
    def _():
        qbd = qbd_ref[...]
        tok = lax.broadcasted_iota(jnp.int32, (nrow, LANES), 0)[:, 0:1] // N_HEADS
        s_new = []
        for t2 in range(n_new):
            kn = kvn_ref[t2:t2 + 1, 0:AW].astype(BF16).astype(F32)
            st = jnp.sum(qbd * kn, axis=-1, keepdims=True)
            cn_t2 = jnp.concatenate([cn_ref[t2 * N_HEADS:(t2 + 1) * N_HEADS, :]] * n_new, axis=0)[:, 0:1]
            st = st + cn - cn_t2
            s_new.append(jnp.where(tok >= t2, st, NEG_BIG))
        m_fin = m_new
        for st in s_new:
            m_fin = jnp.maximum(m_fin, st)
        beta = jnp.exp(m_new - m_fin)
        l_fin = beta * l_new
        acc_fin = beta * acc_new
        for t2 in range(n_new):
            pn = jnp.exp(s_new[t2] - m_fin)
            l_fin = l_fin + pn
            vn = kvn_ref[t2:t2 + 1, AW:2 * AW].astype(BF16).astype(F32)
            acc_fin = acc_fin + pn.astype(BF16).astype(F32) * vn
        out = acc_fin / l_fin
        for t in range(n_new):
            blk = jnp.where(mask16, out[t * N_HEADS:(t + 1) * N_HEADS, :], 0.0)
            o_ref[t:t + 1, :] = jnp.sum(blk, axis=0, keepdims=True)


def fox_sample_attention(page_table, q, kv_new, lf_new, cache_kvt, cache_lft, layer):
    db, t, _ = q.shape
    npg = page_table.shape[0] // db
    page = cache_kvt.shape[3]
    nrow = t * N_HEADS
    pps = FOX_PAGES_PER_STEP
    assert npg % pps == 0

    def page_map(k):
        return lambda b, p, pt: (layer, pt[b * npg + (npg - 1 - (p * pps + k))], 0, 0)

    grid_spec = pltpu.PrefetchScalarGridSpec(
        num_scalar_prefetch=1,
        grid=(db, npg // pps),
        in_specs=[
            pl.BlockSpec((None, t, AW), lambda b, p, pt: (b, 0, 0)),
            pl.BlockSpec((None, t, 2 * AW), lambda b, p, pt: (b, 0, 0)),
            pl.BlockSpec((None, t, N_HEADS), lambda b, p, pt: (b, 0, 0)),
        ] + [pl.BlockSpec((None, None, 2 * AW, page), page_map(k)) for k in range(pps)]
        + [pl.BlockSpec((None, None, N_HEADS, page), page_map(k)) for k in range(pps)],
        out_specs=pl.BlockSpec((None, t, AW), lambda b, p, pt: (b, 0, 0)),
        scratch_shapes=[
            pltpu.VMEM((nrow, AW), F32),
            pltpu.VMEM((nrow, AW), BF16),
            pltpu.VMEM((nrow, LANES), F32),
            pltpu.VMEM((nrow, LANES), F32),
            pltpu.VMEM((nrow, LANES), F32),
            pltpu.VMEM((nrow, AW), F32),
            pltpu.VMEM((N_HEADS, LANES), F32),
        ],
    )
    return pl.pallas_call(
        functools.partial(_fox_sample_kernel, n_new=t, pps=pps),
        grid_spec=grid_spec,
        out_shape=jax.ShapeDtypeStruct((db, t, AW), F32),
        compiler_params=_params("parallel", "arbitrary"),
        name="fox_sample_attention",
    )(page_table, q, kv_new, lf_new, *([cache_kvt] * pps), *([cache_lft] * pps))


def _dilated_prompt_kernel(slope_ref, q_ref, kp_ref, vp_ref, kc_ref, vc_ref, o_ref, lse_ref):
    u = pl.program_id(2)
    tu = q_ref.shape[0]
    lane = lax.broadcasted_iota(jnp.int32, (1, LANES), 1)
    iq = lax.broadcasted_iota(jnp.int32, (tu, 2 * tu), 0)
    ik = lax.broadcasted_iota(jnp.int32, (tu, 2 * tu), 1)
    taps = iq + tu - ik
    dist = taps.astype(F32)
    ok = jnp.logical_and(jnp.logical_and(taps >= 0, taps <= tu), jnp.logical_or(ik >= tu, u > 0))
    lse_all = jnp.zeros((tu, LANES), F32)
    for hp in range(N_PAIRS):
        sl = slice(hp * LANES, (hp + 1) * LANES)
        q = q_ref[:, sl]
        k = jnp.concatenate([kp_ref[:, sl], kc_ref[:, sl]], axis=0)
        v = jnp.concatenate([vp_ref[:, sl], vc_ref[:, sl]], axis=0)
        outs = []
        for hh in range(HEADS_PER_BLOCK):
            head = hp * HEADS_PER_BLOCK + hh
            slope = slope_ref[head]
            s = _dot_nt(_select_head(q, hh), k)
            s = jnp.where(ok, s - slope * dist, NEG_BIG)
            m = jnp.max(s, axis=-1, keepdims=True)
            p = jnp.exp(s - m)
            l = jnp.sum(p, axis=-1, keepdims=True)
            acc = jnp.dot(p.astype(BF16), v, preferred_element_type=F32)
            outs.append(acc / l)
            lse_all = jnp.where(lane == head, m + jnp.log(l), lse_all)
        o_ref[:, sl] = jnp.where(lane < HEAD_DIM, outs[0], outs[1]).astype(o_ref.dtype)
    lse_ref[...] = lse_all


def dilated_prompt_branch(q_g, kv_g, slopes_d, g):
    b, d, su, _ = q_g.shape
    tu = DIL_TILE

    def prev(u):
        return jnp.maximum(u - 1, 0)

    blk = (None, None, tu, AW)
    return pl.pallas_call(
        _dilated_prompt_kernel,
        grid=(b, d, su // tu),
        in_specs=[
            pl.BlockSpec(memory_space=pltpu.SMEM),
            pl.BlockSpec(blk, lambda bi, r, u: (bi, r, u, 0)),
            pl.BlockSpec(blk, lambda bi, r, u: (bi, r, prev(u), 0)),
            pl.BlockSpec(blk, lambda bi, r, u: (bi, r, prev(u), 1)),
            pl.BlockSpec(blk, lambda bi, r, u: (bi, r, u, 0)),
            pl.BlockSpec(blk, lambda bi, r, u: (bi, r, u, 1)),
        ],
        out_specs=[
            pl.BlockSpec(blk, lambda bi, r, u: (bi, r, u, 0)),
            pl.BlockSpec((None, None, tu, LANES), lambda bi, r, u: (bi, r, u, 0)),
        ],
        out_shape=[
            jax.ShapeDtypeStruct((b, d, su, AW), BF16),
            jax.ShapeDtypeStruct((b, d, su, LANES), F32),
        ],
        compiler_params=_params("parallel", "parallel", "arbitrary"),
        name=f"dilated_prompt_branch{g}",
    )(slopes_d, q_g, kv_g, kv_g, kv_g, kv_g)


def _merge_proj_residual_kernel(x_ref, o0_ref, o1_ref, o2_ref, l0_ref, l1_ref, l2_ref,
                                u1_ref, u2_ref, u1f_ref, u2f_ref, e_ref, w_ref, y_ref):
    tm = x_ref.shape[0]
    o_refs, l_refs = (o0_ref, o1_ref, o2_ref), (l0_ref, l1_ref, l2_ref)
    undo_b, undo_f = (None, u1_ref, u2_ref), (None, u1f_ref, u2f_ref)
    outs, lses = [], []
    for g in range(N_BRANCH):
        o = o_refs[g][...].reshape(tm, AW)
        lse = l_refs[g][...].reshape(tm, LANES)
        if undo_b[g] is not None:
            o = jnp.dot(undo_b[g][...], o, preferred_element_type=F32)
            lse = jnp.dot(undo_f[g][...], lse, precision=lax.Precision.HIGHEST, preferred_element_type=F32)
        outs.append(o.astype(F32))
        lses.append(lse)
    m = jnp.maximum(jnp.maximum(lses[0], lses[1]), lses[2])
    es = [jnp.exp(l - m) for l in lses]
    inv = 1.0 / (es[0] + es[1] + es[2])
    merged = jnp.zeros((tm, AW), F32)
    for g in range(N_BRANCH):
        w = es[g] * inv
        w_hi = w.astype(BF16)
        w_lo = (w - w_hi.astype(F32)).astype(BF16)
        w_wide = (jnp.dot(w_hi, e_ref[...], preferred_element_type=F32)
                  + jnp.dot(w_lo, e_ref[...], preferred_element_type=F32))
        merged = merged + w_wide * outs[g]
    y_ref[...] = x_ref[...] + jnp.dot(merged.astype(BF16), w_ref[...], preferred_element_type=F32)


def merge_proj_residual(x, outs, lses, w):
    b, s, d = x.shape
    tm = ROW_TILE
    undo = [_deinterleave_matrix(tm, dil).T for dil in DILATIONS[1:]]
    expand = np.zeros((LANES, AW), np.float32)
    expand[np.arange(AW) // HEAD_DIM, np.arange(AW)] = 1.0
    const = lambda shape: pl.BlockSpec(shape, lambda bi, i: (0,) * len(shape))
    grouped = lambda dil, width: pl.BlockSpec((None, dil, tm // dil, width), lambda bi, i: (bi, 0, i, 0))
    return pl.pallas_call(
        _merge_proj_residual_kernel,
        grid=(b, s // tm),
        in_specs=[pl.BlockSpec((None, tm, d), lambda bi, i: (bi, i, 0))]
        + [grouped(dil, AW) for dil in DILATIONS] + [grouped(dil, LANES) for dil in DILATIONS]
        + [const((tm, tm))] * 4 + [const((LANES, AW)), const((AW, d))],
        out_specs=pl.BlockSpec((None, tm, d), lambda bi, i: (bi, i, 0)),
        out_shape=jax.ShapeDtypeStruct((b, s, d), F32),
        compiler_params=_params("parallel", "parallel"),
        name="merge_proj_residual",
    )(x, *outs, *lses, jnp.asarray(undo[0], BF16), jnp.asarray(undo[1], BF16),
      jnp.asarray(undo[0], F32), jnp.asarray(undo[1], F32), jnp.asarray(expand, BF16), w)


def _dilated_sample_kernel(*refs, n_new, write_cache):
    slope_ref, q_ref, kvn_ref = refs[0:3]
    cache_refs = refs[3:6]
    o_ref = refs[6]
    out_refs = refs[7:10] if write_cache else None
    for pp in range(q_ref.shape[0]):
        _dilated_sample_pair(pp, slope_ref, q_ref, kvn_ref, cache_refs, o_ref, out_refs, n_new)


def _dilated_sample_pair(pp, slope_ref, q_ref, kvn_ref, cache_refs, o_ref, out_refs, n_new):
    feat = slice(pp * LANES, (pp + 1) * LANES)
    nr = q_ref.shape[2]
    row = lax.broadcasted_iota(jnp.int32, (nr, LANES), 0)
    lane = lax.broadcasted_iota(jnp.int32, (nr, LANES), 1)
    qmask = jnp.logical_and(lane // HEAD_DIM == row // n_new, row < HEADS_PER_BLOCK * n_new)
    t_col = (row % n_new)[:, 0:1]
    slope = slope_ref[pp, :, 0:1]

    scores, values, news = [], [], []
    for g in range(N_BRANCH):
        d = DILATIONS[g]
        kt = cache_refs[g][0, feat, :]
        vt = cache_refs[g][1, feat, :]
        length = kt.shape[1]
        qf = jnp.where(qmask, q_ref[pp, g], 0.0)
        s = jnp.dot(qf.astype(BF16), kt.astype(BF16), preferred_element_type=F32)
        pos = lax.broadcasted_iota(jnp.int32, (1, length), 1)
        dist = (length + t_col - pos).astype(F32)
        valid = (pos >= t_col) if d == 1 else ((pos & (d - 1)) == t_col)
        scores.append(jnp.where(valid, s - slope * dist, NEG_BIG))
        values.append(vt.astype(BF16))
        for t2 in range(n_new):
            kn = kvn_ref[pp, g, 0, t2:t2 + 1, :].astype(BF16).astype(F32)
            vn = kvn_ref[pp, g, 1, t2:t2 + 1, :].astype(BF16).astype(F32)
            sn = jnp.sum(qf * kn, axis=-1, keepdims=True)
            ok = (t_col >= t2) if d == 1 else (t_col == t2)
            sn = jnp.where(ok, sn - slope * ((t_col - t2) * d).astype(F32), NEG_BIG)
            news.append((sn, vn))
        if out_refs is not None:
            for c in range(2):
                src = cache_refs[g][c, feat, :]
                new8 = kvn_ref[pp, g, c]
                padded = jnp.concatenate([new8, jnp.zeros((LANES - new8.shape[0], LANES), F32)], axis=0)
                new_cols = pltpu.roll(padded.T, LANES - n_new, axis=1)
                shifted = pltpu.roll(src, length - n_new, axis=1)
                tail_lane = lax.broadcasted_iota(jnp.int32, (LANES, LANES), 1)
                tail = jnp.where(tail_lane >= LANES - n_new, new_cols, shifted[:, length - LANES:])
                if length > LANES:
                    out_refs[g][c, feat, 0:length - LANES] = shifted[:, 0:length - LANES]
                out_refs[g][c, feat, length - LANES:] = tail

    m = jnp.max(scores[0], axis=-1, keepdims=True)
    for s in scores[1:]:
        m = jnp.maximum(m, jnp.max(s, axis=-1, keepdims=True))
    for sn, _ in news:
        m = jnp.maximum(m, sn)
    l = jnp.zeros((nr, 1), F32)
    acc = jnp.zeros((nr, LANES), F32)
    for s, vt in zip(scores, values):
        p = jnp.exp(s - m)
        l = l + jnp.sum(p, axis=-1, keepdims=True)
        acc = acc + _dot_nt(p.astype(BF16), vt)
    for sn, vn in news:
        p = jnp.exp(sn - m)
        l = l + p
        acc = acc + p.astype(BF16).astype(F32) * vn
    out = acc / l
    second = pltpu.roll(out, nr - n_new, axis=0)
    o_ref[pp] = jnp.where(lane < HEAD_DIM, out, second)[0:o_ref.shape[1], :]


def dilated_sample_attention(q16, kvn, slope_rows, caches_t, *, n_new, write_cache):
    db = q16.shape[0]
    nr = q16.shape[3]
    npp = SAMPLE_PAIRS_PER_STEP if write_cache else 2 * SAMPLE_PAIRS_PER_STEP
    in_specs = [
        pl.BlockSpec((npp, nr, LANES), lambda b, hp: (hp, 0, 0)),
        pl.BlockSpec((None, npp, N_BRANCH, nr, LANES), lambda b, hp: (b, hp, 0, 0, 0)),
        pl.BlockSpec((None, npp, N_BRANCH, 2, 8, LANES), lambda b, hp: (b, hp, 0, 0, 0, 0)),
    ]
    cache_specs = [pl.BlockSpec((None, 2, npp * LANES, c.shape[3]), lambda b, hp: (b, 0, hp, 0)) for c in caches_t]
    cache_args = list(caches_t)
    out_specs = [pl.BlockSpec((None, npp, 8, LANES), lambda b, hp: (b, hp, 0, 0))]
    out_shape = [jax.ShapeDtypeStruct((db, N_PAIRS, 8, LANES), F32)]
    if write_cache:
        out_specs += cache_specs
        out_shape += [jax.ShapeDtypeStruct(c.shape, F32) for c in cache_args]
    outs = pl.pallas_call(
        functools.partial(_dilated_sample_kernel, n_new=n_new, write_cache=write_cache),
        grid=(db, N_PAIRS // npp),
        in_specs=in_specs + cache_specs,
        out_specs=out_specs,
        out_shape=out_shape,
        compiler_params=_params("parallel", "arbitrary"),
        name="dilated_sample_attention" + ("_shift" if write_cache else ""),
    )(slope_rows, q16, kvn, *cache_args)
    return outs


def _proj_residual_kernel(x_ref, o_ref, w_ref, y_ref):
    y_ref[...] = x_ref[...] + jnp.dot(o_ref[...].astype(BF16), w_ref[...], preferred_element_type=F32)


def proj_residual(x, o, w):
    rows, d = x.shape
    tm = min(ROW_TILE, rows)
    return pl.pallas_call(
        _proj_residual_kernel,
        grid=(rows // tm,),
        in_specs=[
            pl.BlockSpec((tm, d), lambda i: (i, 0)),
            pl.BlockSpec((tm, AW), lambda i: (i, 0)),
            pl.BlockSpec((AW, d), lambda i: (0, 0)),
        ],
        out_specs=pl.BlockSpec((tm, d), lambda i: (i, 0)),
        out_shape=jax.ShapeDtypeStruct((rows, d), F32),
        compiler_params=_params("parallel"),
        name="proj_residual",
    )(x, o, w)


def _ffn_kernel(x_ref, g_ref, wg_ref, wu_ref, wd_ref, y_ref, h_ref, acc_ref):
    j = pl.program_id(1)

    @pl.when(j == 0)
    def _():
        h_ref[...] = _rmsnorm_bf16(x_ref[...], g_ref[...])
        acc_ref[...] = jnp.zeros_like(acc_ref)

    h = h_ref[...]
    gate = jnp.dot(h, wg_ref[...], preferred_element_type=F32)
    up = jnp.dot(h, wu_ref[...], preferred_element_type=F32)
    act = gate * (1.0 / (1.0 + jnp.exp(-gate))) * up
    acc_ref[...] += jnp.dot(act.astype(BF16), wd_ref[...], preferred_element_type=F32)

    @pl.when(j == pl.num_programs(1) - 1)
    def _():
        y_ref[...] = x_ref[...] + acc_ref[...]


def ffn_residual(x, g, w_gate_up, w_down, *, tm=1024, tf=256):
    rows, d = x.shape
    f = w_down.shape[0]
    tm = min(tm, rows)
    nf = f // tf
    return pl.pallas_call(
        _ffn_kernel,
        grid=(rows // tm, nf),
        in_specs=[
            pl.BlockSpec((tm, d), lambda i, j: (i, 0)),
            pl.BlockSpec((1, d), lambda i, j: (0, 0)),
            pl.BlockSpec((d, tf), lambda i, j: (0, j)),
            pl.BlockSpec((d, tf), lambda i, j: (0, nf + j)),
            pl.BlockSpec((tf, d), lambda i, j: (j, 0)),
        ],
        out_specs=pl.BlockSpec((tm, d), lambda i, j: (i, 0)),
        out_shape=jax.ShapeDtypeStruct((rows, d), F32),
        scratch_shapes=[pltpu.VMEM((tm, d), BF16), pltpu.VMEM((tm, d), F32)],
        compiler_params=_params("parallel", "arbitrary"),
        name="ffn_residual",
    )(x, g.reshape(1, d), w_gate_up, w_gate_up, w_down)


def _rmsnorm_kernel(x_ref, g_ref, y_ref):
    x = x_ref[...]
    y = x * lax.rsqrt(jnp.mean(x * x, axis=-1, keepdims=True) + RMS_EPS)
    y_ref[...] = y * g_ref[...]


def rmsnorm(x, g):
    rows, d = x.shape
    tm = min(ROW_TILE, rows)
    return pl.pallas_call(
        _rmsnorm_kernel,
        grid=(rows // tm,),
        in_specs=[pl.BlockSpec((tm, d), lambda i: (i, 0)), pl.BlockSpec((1, d), lambda i: (0, 0))],
        out_specs=pl.BlockSpec((tm, d), lambda i: (i, 0)),
        out_shape=jax.ShapeDtypeStruct((rows, d), F32),
        compiler_params=_params("parallel"),
        name="rmsnorm",
    )(x, g.reshape(1, d))


def _to_token_major(kv_t, lead):
    n = len(lead)
    x = kv_t.reshape(*lead, 2, N_HEADS, HEAD_DIM, kv_t.shape[-1])
    return x.transpose(*range(n), n + 3, n, n + 1, n + 2)


def kernel(x_prompt, x_sample, cache_fox_kv, cache_fox_logf, cache_win0, cache_win1, cache_win2, page_table,
           norm_g, w_qkvf, b_f, wo_a, norm_kv, w_kv_shared, w_q_b, wo_b, w_gate_up, w_down, norm_final):
    b, s, d = x_prompt.shape
    db, t, _ = x_sample.shape
    n_a = w_qkvf.shape[0]
    depth = norm_g.shape[0]
    n_pool, page = cache_fox_kv.shape[1], cache_fox_kv.shape[2]
    caches = (cache_win0, cache_win1, cache_win2)
    assert d == AW and s % (DILATIONS[-1] * DIL_TILE) == 0 and s % ROW_TILE == 0 and t <= DILATIONS[1]
    assert all(c.shape[1] == w for c, w in zip(caches, WINDOWS)) and WINDOWS[-1] <= s
    assert 2 * t <= 8 and ROW_TILE % (DILATIONS[-1] * 16) == 0

    slopes = jnp.exp2(-8.0 * jnp.arange(1, N_HEADS + 1, dtype=F32) / N_HEADS)
    cache_kvt = cache_fox_kv.transpose(0, 1, 3, 4, 5, 2).reshape(n_a, n_pool, 2 * AW, page)
    cache_lft = cache_fox_logf.transpose(0, 1, 3, 2)
    caches_t = [c.transpose(0, 2, 3, 4, 1).reshape(db, 2, AW, c.shape[1]) for c in caches]
    pt_flat = page_table.reshape(-1)
    slope_rows = jnp.pad(jnp.repeat(slopes.reshape(N_PAIRS, HEADS_PER_BLOCK), t, axis=1),
                         ((0, 0), (0, 16 - HEADS_PER_BLOCK * t)))
    slope_rows = jnp.broadcast_to(slope_rows[:, :, None], (N_PAIRS, 16, LANES))

    xp = x_prompt
    xs = x_sample.reshape(db * t, d)
    fox_kv_p = jnp.zeros((n_a, b, 2 * AW, s), F32)
    fox_lf_p, fox_kv_s, fox_lf_s = [], [], []
    tq = ROW_TILE
    for layer in range(depth):
        g1, g2 = norm_g[layer, 0], norm_g[layer, 1]
        if layer < n_a:
            a = layer
            wt = w_qkvf[a].T
            wt_qkv = wt[:3 * AW].astype(BF16)
            wt_f = jnp.pad(wt[3 * AW:], ((0, LANES - N_HEADS), (0, 0))).astype(BF16)
            wo = wo_a[a].astype(BF16)
            q, fox_kv_p, kvb, lf_t, qn, kn = fox_project_prompt(xp, g1, wt_qkv, wt_f, b_f[a], fox_kv_p, a)
            c, cmax, cmin = cumsum_seq(lf_t)
            first = first_needed_tile(qn[:, :, 0, :N_HEADS], kn[:, :, :, 0], cmax[:, :, :, 0], cmin[:, :, :, 0])
            c = c.reshape(b, N_PAIRS, HEADS_PER_BLOCK, s)
            c_cols = c.transpose(0, 1, 3, 2)
            c_rows = c.reshape(b, N_PAIRS, HEADS_PER_BLOCK, s // tq, tq).transpose(0, 1, 3, 2, 4)
            o = fox_prompt_attention(first, q, kvb, c_cols, c_rows)
            xp = proj_residual(xp.reshape(b * s, d), o.reshape(b * s, AW), wo).reshape(b, s, d)
            fox_lf_p.append(lf_t.transpose(0, 2, 1))
            q, kv, lf = fox_project_sample(xs, g1, wt_qkv, wt_f, b_f[a])
            o = fox_sample_attention(pt_flat, q.astype(F32).reshape(db, t, AW), kv.reshape(db, t, 2 * AW),
                                     lf.reshape(db, t, N_HEADS), cache_kvt, cache_lft, a)
            xs = proj_residual(xs, o.reshape(db * t, AW), wo)
            fox_kv_s.append(kv.reshape(db, t, 2, N_HEADS, HEAD_DIM))
            fox_lf_s.append(lf.reshape(db, t, N_HEADS))
        else:
            bl = layer - n_a
            wq = w_q_b[bl].astype(BF16)
            wo = wo_b[bl].astype(BF16)
            outs, lses = [], []
            for g in range(N_BRANCH):
                q_g = norm_matmul_grouped(xp, g1, wq[:, g * AW:(g + 1) * AW], DILATIONS[g], scale=Q_SCALE)
                o_g, lse_g = dilated_prompt_branch(q_g, kvsh_p[g], slopes * float(DILATIONS[g]), g)
                outs.append(o_g)
                lses.append(lse_g)
            xp = merge_proj_residual(xp, outs, lses, wo)
            q3 = norm_matmul(xs, g1, wq, BF16, scale=Q_SCALE).astype(F32)
            q16 = q3.reshape(db, t, N_BRANCH, N_PAIRS, LANES).transpose(0, 3, 2, 1, 4)
            q16 = jnp.concatenate([q16] * HEADS_PER_BLOCK, axis=3)
            q16 = jnp.pad(q16, ((0, 0),) * 3 + ((0, 16 - HEADS_PER_BLOCK * t), (0, 0)))
            res = dilated_sample_attention(q16, kvn, slope_rows, caches_t, n_new=t, write_cache=(bl == 0))
            if bl == 0:
                win_s = [_to_token_major(res[1 + g], (db,)) for g in range(N_BRANCH)]
            o = res[0][:, :, :t].transpose(0, 2, 1, 3).reshape(db * t, AW)
            xs = proj_residual(xs, o, wo)
        wgu = w_gate_up[layer].astype(BF16)
        wd = w_down[layer].astype(BF16)
        xp = ffn_residual(xp.reshape(b * s, d), g2, wgu, wd).reshape(b, s, d)
        xs = ffn_residual(xs, g2, wgu, wd)
        if layer == n_a - 1:
            wkv = w_kv_shared.astype(BF16)
            kvsh_p = [norm_matmul_grouped(xp, norm_kv, wkv[:, g * 2 * AW:(g + 1) * 2 * AW], DILATIONS[g])
                      for g in range(N_BRANCH)]
            tail = WINDOWS[-1]
            kv_tail = norm_matmul_t(xp[:, s - tail:], norm_kv, w_kv_shared.T.astype(BF16))
            win_p = [_to_token_major(kv_tail[:, g * 2 * AW:(g + 1) * 2 * AW, tail - WINDOWS[g]:], (b,))
                     for g in range(N_BRANCH)]
            kvsh_s = norm_matmul(xs, norm_kv, wkv, F32)
            kvn = kvsh_s.reshape(db, t, N_BRANCH, 2, N_PAIRS, LANES).transpose(0, 4, 2, 3, 1, 5)
            kvn = jnp.pad(kvn, ((0, 0),) * 4 + ((0, 8 - t), (0, 0)))
    y_prompt = rmsnorm(xp.reshape(b * s, d), norm_final).reshape(b, s, d)
    y_sample = rmsnorm(xs, norm_final).reshape(db, t, d)
    return (y_prompt, y_sample, _to_token_major(fox_kv_p, (n_a, b)), jnp.stack(fox_lf_p), jnp.stack(fox_kv_s),
            jnp.stack(fox_lf_s), win_p[0], win_s[0], win_p[1], win_s[1], win_p[2], win_s[2])
```

```python
import functools

import numpy as np
import jax
import jax.numpy as jnp
from jax import lax
from jax.experimental import pallas as pl
from jax.experimental.pallas import tpu as pltpu

F32 = jnp.float32
BF16 = jnp.bfloat16

N_HEADS = 16
HEAD_DIM = 64
AW = N_HEADS * HEAD_DIM
N_BRANCH = 3
WINDOWS = (128, 512, 2048)
DILATIONS = (1, 4, 16)
N_TAPS = 129
RMS_EPS = 1e-6
NEG_BIG = -1e30
LANES = 128
HEADS_PER_BLOCK = LANES // HEAD_DIM
N_PAIRS = N_HEADS // HEADS_PER_BLOCK
Q_SCALE = HEAD_DIM ** -0.5
VMEM_LIMIT = 56 * 1024 * 1024
ROW_TILE = 512
FOX_SUB = 512
DIL_TILE = 128
FOX_PAGES_PER_STEP = 8
SAMPLE_PAIRS_PER_STEP = 2
SKIP_LOGIT = 120.0

_NT = (((1,), (1,)), ((), ()))


def _params(*sem, flags=None):
    return pltpu.CompilerParams(dimension_semantics=sem, vmem_limit_bytes=VMEM_LIMIT, flags=flags)


def _rmsnorm_bf16(x, g):
    y = x * lax.rsqrt(jnp.mean(x * x, axis=-1, keepdims=True) + RMS_EPS)
    return (y * g).astype(BF16)


def _log_sigmoid(x):
    return jnp.minimum(x, 0.0) - jnp.log1p(jnp.exp(-jnp.abs(x)))


def _select_head(q, hh):
    lane = lax.broadcasted_iota(jnp.int32, (1, LANES), 1)
    keep = ((lane // HEAD_DIM) == hh).astype(F32)
    return (q.astype(F32) * keep).astype(BF16)


def _dot_nt(a, b):
    return lax.dot_general(a, b, _NT, preferred_element_type=F32)


def _deinterleave_matrix(n, d):
    p = np.zeros((n, n), np.float32)
    i = np.arange(n)
    p[(i % d) * (n // d) + i // d, i] = 1.0
    return p


def _norm_matmul_kernel(x_ref, g_ref, w_ref, o_ref, h_ref, *, scale):
    @pl.when(pl.program_id(1) == 0)
    def _():
        h_ref[...] = _rmsnorm_bf16(x_ref[...], g_ref[...])

    acc = jnp.dot(h_ref[...], w_ref[...], preferred_element_type=F32)
    o_ref[...] = (acc * scale).astype(o_ref.dtype)


def norm_matmul(x, g, w, out_dtype, *, scale=1.0, tn=1024):
    rows, d = x.shape
    n = w.shape[1]
    tm = min(ROW_TILE, rows)
    return pl.pallas_call(
        functools.partial(_norm_matmul_kernel, scale=scale),
        grid=(rows // tm, n // tn),
        in_specs=[
            pl.BlockSpec((tm, d), lambda i, j: (i, 0)),
            pl.BlockSpec((1, d), lambda i, j: (0, 0)),
            pl.BlockSpec((d, tn), lambda i, j: (0, j)),
        ],
        out_specs=pl.BlockSpec((tm, tn), lambda i, j: (i, j)),
        out_shape=jax.ShapeDtypeStruct((rows, n), out_dtype),
        scratch_shapes=[pltpu.VMEM((tm, d), BF16)],
        compiler_params=_params("parallel", "arbitrary"),
        name="norm_matmul",
    )(x, g.reshape(1, d), w)


def _norm_matmul_t_kernel(x_ref, g_ref, wt_ref, o_ref, h_ref):
    @pl.when(pl.program_id(2) == 0)
    def _():
        h_ref[...] = _rmsnorm_bf16(x_ref[...], g_ref[...])

    o_ref[...] = _dot_nt(wt_ref[...], h_ref[...])


def norm_matmul_t(x, g, w_t, *, tn=1024):
    b, s, d = x.shape
    n = w_t.shape[0]
    tm = min(ROW_TILE, s)
    return pl.pallas_call(
        _norm_matmul_t_kernel,
        grid=(b, s // tm, n // tn),
        in_specs=[
            pl.BlockSpec((None, tm, d), lambda bi, i, j: (bi, i, 0)),
            pl.BlockSpec((1, d), lambda bi, i, j: (0, 0)),
            pl.BlockSpec((tn, d), lambda bi, i, j: (j, 0)),
        ],
        out_specs=pl.BlockSpec((None, tn, tm), lambda bi, i, j: (bi, j, i)),
        out_shape=jax.ShapeDtypeStruct((b, n, s), F32),
        scratch_shapes=[pltpu.VMEM((tm, d), BF16)],
        compiler_params=_params("parallel", "parallel", "arbitrary"),
        name="norm_matmul_t",
    )(x, g.reshape(1, d), w_t)


def _norm_matmul_grouped_kernel(x_ref, g_ref, p_ref, w_ref, o_ref, h_ref, *, scale, dil):
    @pl.when(pl.program_id(2) == 0)
    def _():
        h = _rmsnorm_bf16(x_ref[...], g_ref[...])
        if dil > 1:
            h = jnp.dot(p_ref[...], h, preferred_element_type=F32).astype(BF16)
        h_ref[...] = h

    acc = jnp.dot(h_ref[...], w_ref[...], preferred_element_type=F32) * scale
    o_ref[...] = acc.reshape(o_ref.shape).astype(o_ref.dtype)


def norm_matmul_grouped(x, g, w, dil, *, scale=1.0, tn=1024):
    b, s, d = x.shape
    n = w.shape[1]
    tm = ROW_TILE
    perm = jnp.asarray(_deinterleave_matrix(tm, dil), BF16)
    return pl.pallas_call(
        functools.partial(_norm_matmul_grouped_kernel, scale=scale, dil=dil),
        grid=(b, s // tm, n // tn),
        in_specs=[
            pl.BlockSpec((None, tm, d), lambda bi, i, j: (bi, i, 0)),
            pl.BlockSpec((1, d), lambda bi, i, j: (0, 0)),
            pl.BlockSpec((tm, tm), lambda bi, i, j: (0, 0)),
            pl.BlockSpec((d, tn), lambda bi, i, j: (0, j)),
        ],
        out_specs=pl.BlockSpec((None, dil, tm // dil, tn), lambda bi, i, j: (bi, 0, i, j)),
        out_shape=jax.ShapeDtypeStruct((b, dil, s // dil, n), BF16),
        scratch_shapes=[pltpu.VMEM((tm, d), BF16)],
        compiler_params=_params("parallel", "parallel", "arbitrary"),
        name=f"norm_matmul_grouped{dil}",
    )(x, g.reshape(1, d), perm, w)


def _fox_proj_prompt_kernel(x_ref, g_ref, wt_ref, wft_ref, bf_ref, seg_ref, kv_in_ref,
                            q_ref, kv_ref, kvb_ref, lf_ref, qn_ref, kn_ref):
    del kv_in_ref
    tm = x_ref.shape[0]
    h = _rmsnorm_bf16(x_ref[...], g_ref[...])
    q = (_dot_nt(h, wt_ref[0:AW, :]) * Q_SCALE).astype(BF16)
    q_ref[...] = q
    qf = q.astype(F32)
    sq = qf * qf
    sq_hi = sq.astype(BF16)
    sq_lo = (sq - sq_hi.astype(F32)).astype(BF16)
    q_sq = (jnp.dot(sq_hi, seg_ref[...], preferred_element_type=F32)
            + jnp.dot(sq_lo, seg_ref[...], preferred_element_type=F32))
    qn_ref[...] = jnp.broadcast_to(jnp.sqrt(jnp.max(q_sq, axis=0, keepdims=True)), qn_ref.shape)
    for c in range(2):
        rows = slice(c * AW, (c + 1) * AW)
        kv = _dot_nt(wt_ref[AW + c * AW:2 * AW + c * AW, :], h)
        kv_ref[rows, :] = kv
        kvb = kv.astype(BF16)
        kvb_ref[rows, :] = kvb
        if c == 0:
            kf = kvb.astype(F32)
            k_sq = jnp.sum((kf * kf).reshape(N_HEADS, HEAD_DIM, tm), axis=1)
            kn_ref[...] = jnp.broadcast_to(jnp.sqrt(jnp.max(k_sq, axis=1, keepdims=True)), kn_ref.shape)
    f = _dot_nt(wft_ref[...], h)[0:N_HEADS, :]
    lf_ref[...] = _log_sigmoid(f + bf_ref[...])


def fox_project_prompt(x, g, wt_qkv, wt_f, b_f, kv_out, layer):
    b, s, d = x.shape
    tm = ROW_TILE
    nt = s // tm
    seg = np.zeros((AW, LANES), np.float32)
    seg[np.arange(AW), np.arange(AW) // HEAD_DIM] = 1.0
    return pl.pallas_call(
        _fox_proj_prompt_kernel,
        grid=(b, nt),
        in_specs=[
            pl.BlockSpec((None, tm, d), lambda bi, i: (bi, i, 0)),
            pl.BlockSpec((1, d), lambda bi, i: (0, 0)),
            pl.BlockSpec((3 * AW, d), lambda bi, i: (0, 0)),
            pl.BlockSpec((LANES, d), lambda bi, i: (0, 0)),
            pl.BlockSpec((N_HEADS, 1), lambda bi, i: (0, 0)),
            pl.BlockSpec((AW, LANES), lambda bi, i: (0, 0)),
            pl.BlockSpec(memory_space=pl.ANY),
        ],
        out_specs=[
            pl.BlockSpec((None, tm, AW), lambda bi, i: (bi, i, 0)),
            pl.BlockSpec((None, None, 2 * AW, tm), lambda bi, i: (layer, bi, 0, i)),
            pl.BlockSpec((None, None, 2 * AW, tm), lambda bi, i: (bi, i, 0, 0)),
            pl.BlockSpec((None, N_HEADS, tm), lambda bi, i: (bi, 0, i)),
            pl.BlockSpec((None, None, 8, LANES), lambda bi, i: (bi, i, 0, 0)),
            pl.BlockSpec((None, None, N_HEADS, LANES), lambda bi, i: (bi, i, 0, 0)),
        ],
        out_shape=[
            jax.ShapeDtypeStruct((b, s, AW), BF16),
            jax.ShapeDtypeStruct(kv_out.shape, F32),
            jax.ShapeDtypeStruct((b, nt, 2 * AW, tm), BF16),
            jax.ShapeDtypeStruct((b, N_HEADS, s), F32),
            jax.ShapeDtypeStruct((b, nt, 8, LANES), F32),
            jax.ShapeDtypeStruct((b, nt, N_HEADS, LANES), F32),
        ],
        input_output_aliases={6: 1},
        compiler_params=_params("parallel", "parallel"),
        name="fox_project_prompt",
    )(x, g.reshape(1, d), wt_qkv, wt_f, b_f.reshape(N_HEADS, 1), jnp.asarray(seg, BF16), kv_out)


def _fox_proj_sample_kernel(x_ref, g_ref, wt_ref, wft_ref, bf_ref, q_ref, kv_ref, lf_ref):
    h = _rmsnorm_bf16(x_ref[...], g_ref[...])
    q_ref[...] = (_dot_nt(h, wt_ref[0:AW, :]) * Q_SCALE).astype(BF16)
    kv_ref[...] = _dot_nt(h, wt_ref[AW:3 * AW, :])
    f = _dot_nt(h, wft_ref[...])[:, 0:N_HEADS]
    lf_ref[...] = _log_sigmoid(f + bf_ref[...])


def fox_project_sample(x, g, wt_qkv, wt_f, b_f):
    rows, d = x.shape
    return pl.pallas_call(
        _fox_proj_sample_kernel,
        grid=(1,),
        in_specs=[
            pl.BlockSpec((rows, d), lambda i: (0, 0)),
            pl.BlockSpec((1, d), lambda i: (0, 0)),
            pl.BlockSpec((3 * AW, d), lambda i: (0, 0)),
            pl.BlockSpec((LANES, d), lambda i: (0, 0)),
            pl.BlockSpec((1, N_HEADS), lambda i: (0, 0)),
        ],
        out_specs=[
            pl.BlockSpec((rows, AW), lambda i: (0, 0)),
            pl.BlockSpec((rows, 2 * AW), lambda i: (0, 0)),
            pl.BlockSpec((rows, N_HEADS), lambda i: (0, 0)),
        ],
        out_shape=[
            jax.ShapeDtypeStruct((rows, AW), BF16),
            jax.ShapeDtypeStruct((rows, 2 * AW), F32),
            jax.ShapeDtypeStruct((rows, N_HEADS), F32),
        ],
        compiler_params=_params("arbitrary"),
        name="fox_project_sample",
    )(x, g.reshape(1, d), wt_qkv, wt_f, b_f.reshape(1, N_HEADS))


def _cumsum_kernel(lf_ref, c_ref, cmax_ref, cmin_ref, carry_ref):
    @pl.when(pl.program_id(1) == 0)
    def _():
        carry_ref[...] = jnp.zeros_like(carry_ref)

    ts = lf_ref.shape[1]
    src = lax.broadcasted_iota(jnp.int32, (ts, ts), 0)
    dst = lax.broadcasted_iota(jnp.int32, (ts, ts), 1)
    upper = (src <= dst).astype(F32)
    c = jnp.dot(lf_ref[...], upper, precision=lax.Precision.HIGHEST, preferred_element_type=F32)
    c = c + carry_ref[:, 0:1]
    c_ref[...] = c
    carry_ref[...] = jnp.broadcast_to(c[:, ts - 1:ts], carry_ref.shape)
    cmax_ref[...] = jnp.broadcast_to(jnp.max(c, axis=1, keepdims=True), cmax_ref.shape)
    cmin_ref[...] = jnp.broadcast_to(jnp.min(c, axis=1, keepdims=True), cmin_ref.shape)


def cumsum_seq(lf_t):
    b, h, s = lf_t.shape
    ts = ROW_TILE
    stat = pl.BlockSpec((None, None, h, LANES), lambda bi, i: (bi, i, 0, 0))
    return pl.pallas_call(
        _cumsum_kernel,
        grid=(b, s // ts),
        in_specs=[pl.BlockSpec((None, h, ts), lambda bi, i: (bi, 0, i))],
        out_specs=[pl.BlockSpec((None, h, ts), lambda bi, i: (bi, 0, i)), stat, stat],
        out_shape=[jax.ShapeDtypeStruct((b, h, s), F32)]
        + [jax.ShapeDtypeStruct((b, s // ts, h, LANES), F32)] * 2,
        scratch_shapes=[pltpu.VMEM((h, LANES), F32)],
        compiler_params=_params("parallel", "arbitrary"),
        name="cumsum_seq",
    )(lf_t)


def _fox_prompt_kernel(first_ref, q_ref, kt_ref, vt_ref, cq_ref, ck_ref, o_ref, *, tq, rq):
    i = pl.program_id(2)
    first = first_ref[(pl.program_id(0) * pl.num_programs(1) + pl.program_id(1)) * pl.num_programs(2) + i]
    nsub = tq // rq
    q = q_ref[...]
    lane = lax.broadcasted_iota(jnp.int32, (1, LANES), 1)
    qm = [_select_head(q, hh) for hh in range(HEADS_PER_BLOCK)]
    cq = [cq_ref[:, hh:hh + 1] for hh in range(HEADS_PER_BLOCK)]
    ones = jnp.ones((HEAD_DIM, tq), BF16)

    def tile(j, carry, masked):
        kt = kt_ref[j]
        vt = vt_ref[j]
        vts = [jnp.concatenate([vt[0:HEAD_DIM], ones], axis=0), jnp.concatenate([ones, vt[HEAD_DIM:]], axis=0)]
        new = []
        for hh in range(HEADS_PER_BLOCK):
            ck = ck_ref[j, hh:hh + 1, :]
            subs = []
            for sb in range(nsub):
                m, acc = carry[hh][sb]
                rows = slice(sb * rq, (sb + 1) * rq)
                s = jnp.dot(qm[hh][rows], kt, preferred_element_type=F32) + (cq[hh][rows] - ck)
                if masked:
                    row = lax.broadcasted_iota(jnp.int32, (rq, tq), 0) + sb * rq
                    col = lax.broadcasted_iota(jnp.int32, (rq, tq), 1)
                    s = jnp.where(col <= row, s, NEG_BIG)
                m_new = jnp.maximum(m, jnp.max(s, axis=-1, keepdims=True))
                p = jnp.exp(s - m_new)
                acc = jnp.exp(m - m_new) * acc + _dot_nt(p.astype(BF16), vts[hh])
                subs.append((m_new, acc))
            new.append(tuple(subs))
        return tuple(new)

    init = tuple(tuple((jnp.full((rq, 1), NEG_BIG, F32), jnp.zeros((rq, LANES), F32)) for _ in range(nsub))
                 for _ in range(HEADS_PER_BLOCK))
    carry = lax.fori_loop(first, i, lambda j, c: tile(j, c, False), init)
    carry = tile(i, carry, True)
    for sb in range(nsub):
        acc0, acc1 = carry[0][sb][1], carry[1][sb][1]
        out = jnp.where(lane < HEAD_DIM, acc0 / acc0[:, HEAD_DIM:HEAD_DIM + 1], acc1 / acc1[:, 0:1])
        o_ref[sb * rq:(sb + 1) * rq, :] = out.astype(o_ref.dtype)


def first_needed_tile(qn, kn, cmax, cmin):
    nt = qn.shape[1]
    bound = (qn[:, :, None, :] * (kn[:, None, :, :] + kn[:, :, None, :])
             + cmax[:, :, None, :] - cmin[:, None, :, :])
    needed = bound >= -SKIP_LOGIT
    needed = jnp.any(needed.reshape(needed.shape[:3] + (N_PAIRS, HEADS_PER_BLOCK)), axis=-1)
    tile_j = jnp.arange(nt, dtype=jnp.int32)[None, None, :, None]
    tile_i = jnp.arange(nt, dtype=jnp.int32)[None, :, None, None]
    first = jnp.min(jnp.where(jnp.logical_and(needed, tile_j < tile_i), tile_j, tile_i), axis=2)
    return first.transpose(0, 2, 1).reshape(-1).astype(jnp.int32)


def fox_prompt_attention(first, q, kvb, c_cols, c_rows):
    b, s, _ = q.shape
    tq = ROW_TILE
    nt = s // tq
    grid_spec = pltpu.PrefetchScalarGridSpec(
        num_scalar_prefetch=1,
        grid=(b, N_PAIRS, nt),
        in_specs=[
            pl.BlockSpec((None, tq, LANES), lambda bi, hp, i, f: (bi, i, hp)),
            pl.BlockSpec((None, nt, LANES, tq), lambda bi, hp, i, f: (bi, 0, hp, 0)),
            pl.BlockSpec((None, nt, LANES, tq), lambda bi, hp, i, f: (bi, 0, N_PAIRS + hp, 0)),
            pl.BlockSpec((None, None, tq, HEADS_PER_BLOCK), lambda bi, hp, i, f: (bi, hp, i, 0)),
            pl.BlockSpec((None, None, nt, HEADS_PER_BLOCK, tq), lambda bi, hp, i, f: (bi, hp, 0, 0, 0)),
        ],
        out_specs=pl.BlockSpec((None, tq, LANES), lambda bi, hp, i, f: (bi, i, hp)),
    )
    return pl.pallas_call(
        functools.partial(_fox_prompt_kernel, tq=tq, rq=FOX_SUB),
        grid_spec=grid_spec,
        out_shape=jax.ShapeDtypeStruct((b, s, AW), BF16),
        compiler_params=_params("parallel", "parallel", "arbitrary"),
        name="fox_prompt_attention",
    )(first, q, kvb, kvb, c_cols, c_rows)


def _head_mask(rows):
    head = lax.broadcasted_iota(jnp.int32, (rows, AW), 1) // HEAD_DIM
    return head == lax.broadcasted_iota(jnp.int32, (rows, AW), 0)


def _row_to_col(row):
    eye = (lax.broadcasted_iota(jnp.int32, (N_HEADS, N_HEADS), 0)
           == lax.broadcasted_iota(jnp.int32, (N_HEADS, N_HEADS), 1))
    return jnp.sum(jnp.where(eye, jnp.broadcast_to(row, (N_HEADS, N_HEADS)), 0.0), axis=1, keepdims=True)


def _fox_sample_kernel(pt_ref, q_ref, kvn_ref, lfn_ref, *refs, n_new, pps):
    del pt_ref
    kv_refs, lf_refs, o_ref = refs[0:pps], refs[pps:2 * pps], refs[2 * pps]
    qbd_ref, qbdb_ref, cn_ref, m_ref, l_ref, acc_ref, carry_ref = refs[2 * pps + 1:]
    p = pl.program_id(1)
    nrow = n_new * N_HEADS
    mask16 = _head_mask(N_HEADS)

    @pl.when(p == 0)
    def _():
        run = jnp.zeros((1, N_HEADS), F32)
        for t in range(n_new):
            rows = slice(t * N_HEADS, (t + 1) * N_HEADS)
            qt = jnp.where(mask16, jnp.broadcast_to(q_ref[t:t + 1, :], (N_HEADS, AW)), 0.0)
            qbd_ref[rows, :] = qt
            qbdb_ref[rows, :] = qt.astype(BF16)
            run = run + lfn_ref[t:t + 1, :]
            cn_ref[rows, :] = jnp.broadcast_to(_row_to_col(run), (N_HEADS, LANES))
        m_ref[...] = jnp.full_like(m_ref, NEG_BIG)
        l_ref[...] = jnp.zeros_like(l_ref)
        acc_ref[...] = jnp.zeros_like(acc_ref)
        carry_ref[...] = jnp.zeros_like(carry_ref)

    page = lf_refs[0].shape[1]
    src = lax.broadcasted_iota(jnp.int32, (page, page), 0)
    dst = lax.broadcasted_iota(jnp.int32, (page, page), 1)
    later = (src > dst).astype(F32)
    cn = cn_ref[:, 0:1]
    qbd_b = qbdb_ref[...]
    carry = carry_ref[:, 0:1]
    scores, values = [], []
    for k in range(pps):
        lf_t = lf_refs[k][...]
        suffix = jnp.dot(lf_t, later, precision=lax.Precision.HIGHEST, preferred_element_type=F32) + carry
        carry = suffix[:, 0:1] + lf_t[:, 0:1]
        kt = kv_refs[k][0:AW, :].astype(BF16)
        s = jnp.dot(qbd_b, kt, preferred_element_type=F32)
        scores.append(s + (jnp.concatenate([suffix] * n_new, axis=0) + cn))
        values.append(kv_refs[k][AW:2 * AW, :].astype(BF16))
    carry_ref[...] = jnp.broadcast_to(carry, carry_ref.shape)
    m_old = m_ref[:, 0:1]
    m_new = m_old
    for s in scores:
        m_new = jnp.maximum(m_new, jnp.max(s, axis=-1, keepdims=True))
    alpha = jnp.exp(m_old - m_new)
    l_new = alpha * l_ref[:, 0:1]
    acc_new = alpha * acc_ref[...]
    for s, vt in zip(scores, values):
        pr = jnp.exp(s - m_new)
        l_new = l_new + jnp.sum(pr, axis=-1, keepdims=True)
        acc_new = acc_new + _dot_nt(pr.astype(BF16), vt)
    m_ref[...] = jnp.broadcast_to(m_new, m_ref.shape)
    l_ref[...] = jnp.broadcast_to(l_new, l_ref.shape)
    acc_ref[...] = acc_new

    @pl.when(p == pl.num_programs(1) - 1)
    def _():
        qbd = qbd_ref[...]
        tok = lax.broadcasted_iota(jnp.int32, (nrow, LANES), 0)[:, 0:1] // N_HEADS
        s_new = []
        for t2 in range(n_new):
            kn = kvn_ref[t2:t2 + 1, 0:AW].astype(BF16).astype(F32)
            st = jnp.sum(qbd * kn, axis=-1, keepdims=True)
            cn_t2 = jnp.concatenate([cn_ref[t2 * N_HEADS:(t2 + 1) * N_HEADS, :]] * n_new, axis=0)[:, 0:1]
            st = st + cn - cn_t2
            s_new.append(jnp.where(tok >= t2, st, NEG_BIG))
        m_fin = m_new
        for st in s_new:
            m_fin = jnp.maximum(m_fin, st)
        beta = jnp.exp(m_new - m_fin)
        l_fin = beta * l_new
        acc_fin = beta * acc_new
        for t2 in range(n_new):
            pn = jnp.exp(s_new[t2] - m_fin)
            l_fin = l_fin + pn
            vn = kvn_ref[t2:t2 + 1, AW:2 * AW].astype(BF16).astype(F32)
            acc_fin = acc_fin + pn.astype(BF16).astype(F32) * vn
        out = acc_fin / l_fin
        for t in range(n_new):
            blk = jnp.where(mask16, out[t * N_HEADS:(t + 1) * N_HEADS, :], 0.0)
            o_ref[t:t + 1, :] = jnp.sum(blk, axis=0, keepdims=True)


def fox_sample_attention(page_table, q, kv_new, lf_new, cache_kvt, cache_lft, layer):
    db, t, _ = q.shape
    npg = page_table.shape[0] // db
    page = cache_kvt.shape[3]
    nrow = t * N_HEADS
    pps = FOX_PAGES_PER_STEP
    assert npg % pps == 0

    def page_map(k):
        return lambda b, p, pt: (layer, pt[b * npg + (npg - 1 - (p * pps + k))], 0, 0)

    grid_spec = pltpu.PrefetchScalarGridSpec(
        num_scalar_prefetch=1,
        grid=(db, npg // pps),
        in_specs=[
            pl.BlockSpec((None, t, AW), lambda b, p, pt: (b, 0, 0)),
            pl.BlockSpec((None, t, 2 * AW), lambda b, p, pt: (b, 0, 0)),
            pl.BlockSpec((None, t, N_HEADS), lambda b, p, pt: (b, 0, 0)),
        ] + [pl.BlockSpec((None, None, 2 * AW, page), page_map(k)) for k in range(pps)]
        + [pl.BlockSpec((None, None, N_HEADS, page), page_map(k)) for k in range(pps)],
        out_specs=pl.BlockSpec((None, t, AW), lambda b, p, pt: (b, 0, 0)),
        scratch_shapes=[
            pltpu.VMEM((nrow, AW), F32),
            pltpu.VMEM((nrow, AW), BF16),
            pltpu.VMEM((nrow, LANES), F32),
            pltpu.VMEM((nrow, LANES), F32),
            pltpu.VMEM((nrow, LANES), F32),
            pltpu.VMEM((nrow, AW), F32),
            pltpu.VMEM((N_HEADS, LANES), F32),
        ],
    )
    return pl.pallas_call(
        functools.partial(_fox_sample_kernel, n_new=t, pps=pps),
        grid_spec=grid_spec,
        out_shape=jax.ShapeDtypeStruct((db, t, AW), F32),
        compiler_params=_params("parallel", "arbitrary"),
        name="fox_sample_attention",
    )(page_table, q, kv_new, lf_new, *([cache_kvt] * pps), *([cache_lft] * pps))


def _dilated_prompt_kernel(slope_ref, q_ref, kp_ref, vp_ref, kc_ref, vc_ref, o_ref, lse_ref):
    u = pl.program_id(2)
    tu = q_ref.shape[0]
    lane = lax.broadcasted_iota(jnp.int32, (1, LANES), 1)
    iq = lax.broadcasted_iota(jnp.int32, (tu, 2 * tu), 0)
    ik = lax.broadcasted_iota(jnp.int32, (tu, 2 * tu), 1)
    taps = iq + tu - ik
    dist = taps.astype(F32)
    ok = jnp.logical_and(jnp.logical_and(taps >= 0, taps <= tu), jnp.logical_or(ik >= tu, u > 0))
    lse_all = jnp.zeros((tu, LANES), F32)
    for hp in range(N_PAIRS):
        sl = slice(hp * LANES, (hp + 1) * LANES)
        q = q_ref[:, sl]
        k = jnp.concatenate([kp_ref[:, sl], kc_ref[:, sl]], axis=0)
        v = jnp.concatenate([vp_ref[:, sl], vc_ref[:, sl]], axis=0)
        outs = []
        for hh in range(HEADS_PER_BLOCK):
            head = hp * HEADS_PER_BLOCK + hh
            slope = slope_ref[head]
            s = _dot_nt(_select_head(q, hh), k)
            s = jnp.where(ok, s - slope * dist, NEG_BIG)
            m = jnp.max(s, axis=-1, keepdims=True)
            p = jnp.exp(s - m)
            l = jnp.sum(p, axis=-1, keepdims=True)
            acc = jnp.dot(p.astype(BF16), v, preferred_element_type=F32)
            outs.append(acc / l)
            lse_all = jnp.where(lane == head, m + jnp.log(l), lse_all)
        o_ref[:, sl] = jnp.where(lane < HEAD_DIM, outs[0], outs[1]).astype(o_ref.dtype)
    lse_ref[...] = lse_all


def dilated_prompt_branch(q_g, kv_g, slopes_d, g):
    b, d, su, _ = q_g.shape
    tu = DIL_TILE

    def prev(u):
        return jnp.maximum(u - 1, 0)

    blk = (None, None, tu, AW)
    return pl.pallas_call(
        _dilated_prompt_kernel,
        grid=(b, d, su // tu),
        in_specs=[
            pl.BlockSpec(memory_space=pltpu.SMEM),
            pl.BlockSpec(blk, lambda bi, r, u: (bi, r, u, 0)),
            pl.BlockSpec(blk, lambda bi, r, u: (bi, r, prev(u), 0)),
            pl.BlockSpec(blk, lambda bi, r, u: (bi, r, prev(u), 1)),
            pl.BlockSpec(blk, lambda bi, r, u: (bi, r, u, 0)),
            pl.BlockSpec(blk, lambda bi, r, u: (bi, r, u, 1)),
        ],
        out_specs=[
            pl.BlockSpec(blk, lambda bi, r, u: (bi, r, u, 0)),
            pl.BlockSpec((None, None, tu, LANES), lambda bi, r, u: (bi, r, u, 0)),
        ],
        out_shape=[
            jax.ShapeDtypeStruct((b, d, su, AW), BF16),
            jax.ShapeDtypeStruct((b, d, su, LANES), F32),
        ],
        compiler_params=_params("parallel", "parallel", "arbitrary"),
        name=f"dilated_prompt_branch{g}",
    )(slopes_d, q_g, kv_g, kv_g, kv_g, kv_g)


def _merge_proj_residual_kernel(x_ref, o0_ref, o1_ref, o2_ref, l0_ref, l1_ref, l2_ref,
                                u1_ref, u2_ref, u1f_ref, u2f_ref, e_ref, w_ref, y_ref):
    tm = x_ref.shape[0]
    o_refs, l_refs = (o0_ref, o1_ref, o2_ref), (l0_ref, l1_ref, l2_ref)
    undo_b, undo_f = (None, u1_ref, u2_ref), (None, u1f_ref, u2f_ref)
    outs, lses = [], []
    for g in range(N_BRANCH):
        o = o_refs[g][...].reshape(tm, AW)
        lse = l_refs[g][...].reshape(tm, LANES)
        if undo_b[g] is not None:
            o = jnp.dot(undo_b[g][...], o, preferred_element_type=F32)
            lse = jnp.dot(undo_f[g][...], lse, precision=lax.Precision.HIGHEST, preferred_element_type=F32)
        outs.append(o.astype(F32))
        lses.append(lse)
    m = jnp.maximum(jnp.maximum(lses[0], lses[1]), lses[2])
    es = [jnp.exp(l - m) for l in lses]
    inv = 1.0 / (es[0] + es[1] + es[2])
    merged = jnp.zeros((tm, AW), F32)
    for g in range(N_BRANCH):
        w = es[g] * inv
        w_hi = w.astype(BF16)
        w_lo = (w - w_hi.astype(F32)).astype(BF16)
        w_wide = (jnp.dot(w_hi, e_ref[...], preferred_element_type=F32)
                  + jnp.dot(w_lo, e_ref[...], preferred_element_type=F32))
        merged = merged + w_wide * outs[g]
    y_ref[...] = x_ref[...] + jnp.dot(merged.astype(BF16), w_ref[...], preferred_element_type=F32)


def merge_proj_residual(x, outs, lses, w):
    b, s, d = x.shape
    tm = ROW_TILE
    undo = [_deinterleave_matrix(tm, dil).T for dil in DILATIONS[1:]]
    expand = np.zeros((LANES, AW), np.float32)
    expand[np.arange(AW) // HEAD_DIM, np.arange(AW)] = 1.0
    const = lambda shape: pl.BlockSpec(shape, lambda bi, i: (0,) * len(shape))
    grouped = lambda dil, width: pl.BlockSpec((None, dil, tm // dil, width), lambda bi, i: (bi, 0, i, 0))
    return pl.pallas_call(
        _merge_proj_residual_kernel,
        grid=(b, s // tm),
        in_specs=[pl.BlockSpec((None, tm, d), lambda bi, i: (bi, i, 0))]
        + [grouped(dil, AW) for dil in DILATIONS] + [grouped(dil, LANES) for dil in DILATIONS]
        + [const((tm, tm))] * 4 + [const((LANES, AW)), const((AW, d))],
        out_specs=pl.BlockSpec((None, tm, d), lambda bi, i: (bi, i, 0)),
        out_shape=jax.ShapeDtypeStruct((b, s, d), F32),
        compiler_params=_params("parallel", "parallel"),
        name="merge_proj_residual",
    )(x, *outs, *lses, jnp.asarray(undo[0], BF16), jnp.asarray(undo[1], BF16),
      jnp.asarray(undo[0], F32), jnp.asarray(undo[1], F32), jnp.asarray(expand, BF16), w)


def _dilated_sample_kernel(*refs, n_new, write_cache):
    slope_ref, q_ref, kvn_ref = refs[0:3]
    cache_refs = refs[3:6]
    o_ref = refs[6]
    out_refs = refs[7:10] if write_cache else None
    for pp in range(q_ref.shape[0]):
        _dilated_sample_pair(pp, slope_ref, q_ref, kvn_ref, cache_refs, o_ref, out_refs, n_new)


def _dilated_sample_pair(pp, slope_ref, q_ref, kvn_ref, cache_refs, o_ref, out_refs, n_new):
    feat = slice(pp * LANES, (pp + 1) * LANES)
    nr = q_ref.shape[2]
    row = lax.broadcasted_iota(jnp.int32, (nr, LANES), 0)
    lane = lax.broadcasted_iota(jnp.int32, (nr, LANES), 1)
    qmask = jnp.logical_and(lane // HEAD_DIM == row // n_new, row < HEADS_PER_BLOCK * n_new)
    t_col = (row % n_new)[:, 0:1]
    slope = slope_ref[pp, :, 0:1]

    scores, values, news = [], [], []
    for g in range(N_BRANCH):
        d = DILATIONS[g]
        kt = cache_refs[g][0, feat, :]
        vt = cache_refs[g][1, feat, :]
        length = kt.shape[1]
        qf = jnp.where(qmask, q_ref[pp, g], 0.0)
        s = jnp.dot(qf.astype(BF16), kt.astype(BF16), preferred_element_type=F32)
        pos = lax.broadcasted_iota(jnp.int32, (1, length), 1)
        dist = (length + t_col - pos).astype(F32)
        valid = (pos >= t_col) if d == 1 else ((pos & (d - 1)) == t_col)
        scores.append(jnp.where(valid, s - slope * dist, NEG_BIG))
        values.append(vt.astype(BF16))
        for t2 in range(n_new):
            kn = kvn_ref[pp, g, 0, t2:t2 + 1, :].astype(BF16).astype(F32)
            vn = kvn_ref[pp, g, 1, t2:t2 + 1, :].astype(BF16).astype(F32)
            sn = jnp.sum(qf * kn, axis=-1, keepdims=True)
            ok = (t_col >= t2) if d == 1 else (t_col == t2)
            sn = jnp.where(ok, sn - slope * ((t_col - t2) * d).astype(F32), NEG_BIG)
            news.append((sn, vn))
        if out_refs is not None:
            for c in range(2):
                src = cache_refs[g][c, feat, :]
                new8 = kvn_ref[pp, g, c]
                padded = jnp.concatenate([new8, jnp.zeros((LANES - new8.shape[0], LANES), F32)], axis=0)
                new_cols = pltpu.roll(padded.T, LANES - n_new, axis=1)
                shifted = pltpu.roll(src, length - n_new, axis=1)
                tail_lane = lax.broadcasted_iota(jnp.int32, (LANES, LANES), 1)
                tail = jnp.where(tail_lane >= LANES - n_new, new_cols, shifted[:, length - LANES:])
                if length > LANES:
                    out_refs[g][c, feat, 0:length - LANES] = shifted[:, 0:length - LANES]
                out_refs[g][c, feat, length - LANES:] = tail

    m = jnp.max(scores[0], axis=-1, keepdims=True)
    for s in scores[1:]:
        m = jnp.maximum(m, jnp.max(s, axis=-1, keepdims=True))
    for sn, _ in news:
        m = jnp.maximum(m, sn)
    l = jnp.zeros((nr, 1), F32)
    acc = jnp.zeros((nr, LANES), F32)
    for s, vt in zip(scores, values):
        p = jnp.exp(s - m)
        l = l + jnp.sum(p, axis=-1, keepdims=True)
        acc = acc + _dot_nt(p.astype(BF16), vt)
    for sn, vn in news:
        p = jnp.exp(sn - m)
        l = l + p
        acc = acc + p.astype(BF16).astype(F32) * vn
    out = acc / l
    second = pltpu.roll(out, nr - n_new, axis=0)
    o_ref[pp] = jnp.where(lane < HEAD_DIM, out, second)[0:o_ref.shape[1], :]


def dilated_sample_attention(q16, kvn, slope_rows, caches_t, *, n_new, write_cache):
    db = q16.shape[0]
    nr = q16.shape[3]
    npp = SAMPLE_PAIRS_PER_STEP if write_cache else 2 * SAMPLE_PAIRS_PER_STEP
    in_specs = [
        pl.BlockSpec((npp, nr, LANES), lambda b, hp: (hp, 0, 0)),
        pl.BlockSpec((None, npp, N_BRANCH, nr, LANES), lambda b, hp: (b, hp, 0, 0, 0)),
        pl.BlockSpec((None, npp, N_BRANCH, 2, 8, LANES), lambda b, hp: (b, hp, 0, 0, 0, 0)),
    ]
    cache_specs = [pl.BlockSpec((None, 2, npp * LANES, c.shape[3]), lambda b, hp: (b, 0, hp, 0)) for c in caches_t]
    cache_args = list(caches_t)
    out_specs = [pl.BlockSpec((None, npp, 8, LANES), lambda b, hp: (b, hp, 0, 0))]
    out_shape = [jax.ShapeDtypeStruct((db, N_PAIRS, 8, LANES), F32)]
    if write_cache:
        out_specs += cache_specs
        out_shape += [jax.ShapeDtypeStruct(c.shape, F32) for c in cache_args]
    outs = pl.pallas_call(
        functools.partial(_dilated_sample_kernel, n_new=n_new, write_cache=write_cache),
        grid=(db, N_PAIRS // npp),
        in_specs=in_specs + cache_specs,
        out_specs=out_specs,
        out_shape=out_shape,
        compiler_params=_params("parallel", "arbitrary"),
        name="dilated_sample_attention" + ("_shift" if write_cache else ""),
    )(slope_rows, q16, kvn, *cache_args)
    return outs


def _proj_residual_kernel(x_ref, o_ref, w_ref, y_ref):
    y_ref[...] = x_ref[...] + jnp.dot(o_ref[...].astype(BF16), w_ref[...], preferred_element_type=F32)


def proj_residual(x, o, w):
    rows, d = x.shape
    tm = min(ROW_TILE, rows)
    return pl.pallas_call(
        _proj_residual_kernel,
        grid=(rows // tm,),
        in_specs=[
            pl.BlockSpec((tm, d), lambda i: (i, 0)),
            pl.BlockSpec((tm, AW), lambda i: (i, 0)),
            pl.BlockSpec((AW, d), lambda i: (0, 0)),
        ],
        out_specs=pl.BlockSpec((tm, d), lambda i: (i, 0)),
        out_shape=jax.ShapeDtypeStruct((rows, d), F32),
        compiler_params=_params("parallel"),
        name="proj_residual",
    )(x, o, w)


def _ffn_kernel(x_ref, g_ref, wg_ref, wu_ref, wd_ref, y_ref, h_ref, acc_ref):
    j = pl.program_id(1)

    @pl.when(j == 0)
    def _():
        h_ref[...] = _rmsnorm_bf16(x_ref[...], g_ref[...])
        acc_ref[...] = jnp.zeros_like(acc_ref)

    h = h_ref[...]
    gate = jnp.dot(h, wg_ref[...], preferred_element_type=F32)
    up = jnp.dot(h, wu_ref[...], preferred_element_type=F32)
    act = gate * (1.0 / (1.0 + jnp.exp(-gate))) * up
    acc_ref[...] += jnp.dot(act.astype(BF16), wd_ref[...], preferred_element_type=F32)

    @pl.when(j == pl.num_programs(1) - 1)
    def _():
        y_ref[...] = x_ref[...] + acc_ref[...]


def ffn_residual(x, g, w_gate_up, w_down, *, tm=512, tf=1408):
    rows, d = x.shape
    f = w_down.shape[0]
    tm = min(tm, rows)
    nf = f // tf
    return pl.pallas_call(
        _ffn_kernel,
        grid=(rows // tm, nf),
        in_specs=[
            pl.BlockSpec((tm, d), lambda i, j: (i, 0)),
            pl.BlockSpec((1, d), lambda i, j: (0, 0)),
            pl.BlockSpec((d, tf), lambda i, j: (0, j)),
            pl.BlockSpec((d, tf), lambda i, j: (0, nf + j)),
            pl.BlockSpec((tf, d), lambda i, j: (j, 0)),
        ],
        out_specs=pl.BlockSpec((tm, d), lambda i, j: (i, 0)),
        out_shape=jax.ShapeDtypeStruct((rows, d), F32),
        scratch_shapes=[pltpu.VMEM((tm, d), BF16), pltpu.VMEM((tm, d), F32)],
        compiler_params=_params("parallel", "arbitrary"),
        name="ffn_residual",
    )(x, g.reshape(1, d), w_gate_up, w_gate_up, w_down)


def _rmsnorm_kernel(x_ref, g_ref, y_ref):
    x = x_ref[...]
    y = x * lax.rsqrt(jnp.mean(x * x, axis=-1, keepdims=True) + RMS_EPS)
    y_ref[...] = y * g_ref[...]


def rmsnorm(x, g):
    rows, d = x.shape
    tm = min(ROW_TILE, rows)
    return pl.pallas_call(
        _rmsnorm_kernel,
        grid=(rows // tm,),
        in_specs=[pl.BlockSpec((tm, d), lambda i: (i, 0)), pl.BlockSpec((1, d), lambda i: (0, 0))],
        out_specs=pl.BlockSpec((tm, d), lambda i: (i, 0)),
        out_shape=jax.ShapeDtypeStruct((rows, d), F32),
        compiler_params=_params("parallel"),
        name="rmsnorm",
    )(x, g.reshape(1, d))


def _to_token_major(kv_t, lead):
    n = len(lead)
    x = kv_t.reshape(*lead, 2, N_HEADS, HEAD_DIM, kv_t.shape[-1])
    return x.transpose(*range(n), n + 3, n, n + 1, n + 2)


def kernel(x_prompt, x_sample, cache_fox_kv, cache_fox_logf, cache_win0, cache_win1, cache_win2, page_table,
           norm_g, w_qkvf, b_f, wo_a, norm_kv, w_kv_shared, w_q_b, wo_b, w_gate_up, w_down, norm_final):
    b, s, d = x_prompt.shape
    db, t, _ = x_sample.shape
    n_a = w_qkvf.shape[0]
    depth = norm_g.shape[0]
    n_pool, page = cache_fox_kv.shape[1], cache_fox_kv.shape[2]
    caches = (cache_win0, cache_win1, cache_win2)
    assert d == AW and s % (DILATIONS[-1] * DIL_TILE) == 0 and s % ROW_TILE == 0 and t <= DILATIONS[1]
    assert all(c.shape[1] == w for c, w in zip(caches, WINDOWS)) and WINDOWS[-1] <= s
    assert 2 * t <= 8 and ROW_TILE % (DILATIONS[-1] * 16) == 0

    slopes = jnp.exp2(-8.0 * jnp.arange(1, N_HEADS + 1, dtype=F32) / N_HEADS)
    cache_kvt = cache_fox_kv.transpose(0, 1, 3, 4, 5, 2).reshape(n_a, n_pool, 2 * AW, page)
    cache_lft = cache_fox_logf.transpose(0, 1, 3, 2)
    caches_t = [c.transpose(0, 2, 3, 4, 1).reshape(db, 2, AW, c.shape[1]) for c in caches]
    pt_flat = page_table.reshape(-1)
    slope_rows = jnp.pad(jnp.repeat(slopes.reshape(N_PAIRS, HEADS_PER_BLOCK), t, axis=1),
                         ((0, 0), (0, 16 - HEADS_PER_BLOCK * t)))
    slope_rows = jnp.broadcast_to(slope_rows[:, :, None], (N_PAIRS, 16, LANES))

    xp = x_prompt
    xs = x_sample.reshape(db * t, d)
    fox_kv_p = jnp.zeros((n_a, b, 2 * AW, s), F32)
    fox_lf_p, fox_kv_s, fox_lf_s = [], [], []
    tq = ROW_TILE
    for layer in range(depth):
        g1, g2 = norm_g[layer, 0], norm_g[layer, 1]
        if layer < n_a:
            a = layer
            wt = w_qkvf[a].T
            wt_qkv = wt[:3 * AW].astype(BF16)
            wt_f = jnp.pad(wt[3 * AW:], ((0, LANES - N_HEADS), (0, 0))).astype(BF16)
            wo = wo_a[a].astype(BF16)
            q, fox_kv_p, kvb, lf_t, qn, kn = fox_project_prompt(xp, g1, wt_qkv, wt_f, b_f[a], fox_kv_p, a)
            c, cmax, cmin = cumsum_seq(lf_t)
            first = first_needed_tile(qn[:, :, 0, :N_HEADS], kn[:, :, :, 0], cmax[:, :, :, 0], cmin[:, :, :, 0])
            c = c.reshape(b, N_PAIRS, HEADS_PER_BLOCK, s)
            c_cols = c.transpose(0, 1, 3, 2)
            c_rows = c.reshape(b, N_PAIRS, HEADS_PER_BLOCK, s // tq, tq).transpose(0, 1, 3, 2, 4)
            o = fox_prompt_attention(first, q, kvb, c_cols, c_rows)
            xp = proj_residual(xp.reshape(b * s, d), o.reshape(b * s, AW), wo).reshape(b, s, d)
            fox_lf_p.append(lf_t.transpose(0, 2, 1))
            q, kv, lf = fox_project_sample(xs, g1, wt_qkv, wt_f, b_f[a])
            o = fox_sample_attention(pt_flat, q.astype(F32).reshape(db, t, AW), kv.reshape(db, t, 2 * AW),
                                     lf.reshape(db, t, N_HEADS), cache_kvt, cache_lft, a)
            xs = proj_residual(xs, o.reshape(db * t, AW), wo)
            fox_kv_s.append(kv.reshape(db, t, 2, N_HEADS, HEAD_DIM))
            fox_lf_s.append(lf.reshape(db, t, N_HEADS))
        else:
            bl = layer - n_a
            wq = w_q_b[bl].astype(BF16)
            wo = wo_b[bl].astype(BF16)
            outs, lses = [], []
            for g in range(N_BRANCH):
                q_g = norm_matmul_grouped(xp, g1, wq[:, g * AW:(g + 1) * AW], DILATIONS[g], scale=Q_SCALE)
                o_g, lse_g = dilated_prompt_branch(q_g, kvsh_p[g], slopes * float(DILATIONS[g]), g)
                outs.append(o_g)
                lses.append(lse_g)
            xp = merge_proj_residual(xp, outs, lses, wo)
            q3 = norm_matmul(xs, g1, wq, BF16, scale=Q_SCALE).astype(F32)
            q16 = q3.reshape(db, t, N_BRANCH, N_PAIRS, LANES).transpose(0, 3, 2, 1, 4)
            q16 = jnp.concatenate([q16] * HEADS_PER_BLOCK, axis=3)
            q16 = jnp.pad(q16, ((0, 0),) * 3 + ((0, 16 - HEADS_PER_BLOCK * t), (0, 0)))
            res = dilated_sample_attention(q16, kvn, slope_rows, caches_t, n_new=t, write_cache=(bl == 0))
            if bl == 0:
                win_s = [_to_token_major(res[1 + g], (db,)) for g in range(N_BRANCH)]
            o = res[0][:, :, :t].transpose(0, 2, 1, 3).reshape(db * t, AW)
            xs = proj_residual(xs, o, wo)
        wgu = w_gate_up[layer].astype(BF16)
        wd = w_down[layer].astype(BF16)
        xp = ffn_residual(xp.reshape(b * s, d), g2, wgu, wd).reshape(b, s, d)
        xs = ffn_residual(xs, g2, wgu, wd)
        if layer == n_a - 1:
            wkv = w_kv_shared.astype(BF16)
            kvsh_p = [norm_matmul_grouped(xp, norm_kv, wkv[:, g * 2 * AW:(g + 1) * 2 * AW], DILATIONS[g])
                      for g in range(N_BRANCH)]
            tail = WINDOWS[-1]
            kv_tail = norm_matmul_t(xp[:, s - tail:], norm_kv, w_kv_shared.T.astype(BF16))
            win_p = [_to_token_major(kv_tail[:, g * 2 * AW:(g + 1) * 2 * AW, tail - WINDOWS[g]:], (b,))
                     for g in range(N_BRANCH)]
            kvsh_s = norm_matmul(xs, norm_kv, wkv, F32)
            kvn = kvsh_s.reshape(db, t, N_BRANCH, 2, N_PAIRS, LANES).transpose(0, 4, 2, 3, 1, 5)
            kvn = jnp.pad(kvn, ((0, 0),) * 4 + ((0, 8 - t), (0, 0)))
    y_prompt = rmsnorm(xp.reshape(b * s, d), norm_final).reshape(b, s, d)
    y_sample = rmsnorm(xs, norm_final).reshape(db, t, d)
    return (y_prompt, y_sample, _to_token_major(fox_kv_p, (n_a, b)), jnp.stack(fox_lf_p), jnp.stack(fox_kv_s),
            jnp.stack(fox_lf_s), win_p[0], win_s[0], win_p[1], win_s[1], win_p[2], win_s[2])
```
